```python
import math
import jax, jax.numpy as jnp
from jax import lax
import numpy as np

D_MODEL = 1024
BATCH = 4
SEQ = 8192
DEPTH = 2

N_MIXERS = 2
ATTN_HEADS = 8
ATTN_HEAD_DIM = D_MODEL // ATTN_HEADS
MOBA_BLOCK = 256
MOBA_TOPK = 3
Q_CHUNK = 128
REL_BUCKETS = 32
REL_MAX_DIST = 1024
GDN_HEADS = 8
GDN_HEAD_DIM = D_MODEL // GDN_HEADS
GDN_CONV = 4
GDN_CHUNK = 64
D_FF = 2816
FFN_CONV = 3
NORM_EPS = 1e-6
L2_EPS = 1e-6

kernel_name = "moba_gdn_convffn_hybrid"


def rmsnorm(x, g):
    xf = x.astype(jnp.float32)
    y = xf * lax.rsqrt(jnp.mean(xf * xf, axis=-1, keepdims=True) + NORM_EPS)
    return (y * g.astype(jnp.float32)).astype(x.dtype)


def l2norm(x):
    return x * lax.rsqrt(jnp.sum(x * x, axis=-1, keepdims=True) + L2_EPS)


def causal_dwconv(x, w):
    K, C = w.shape
    return lax.conv_general_dilated(
        x, w[:, None, :].astype(x.dtype), window_strides=(1,), padding=[(K - 1, 0)],
        dimension_numbers=("NWC", "WIO", "NWC"), feature_group_count=C)


def rel_bucket(dist):
    max_exact = REL_BUCKETS // 2
    d = jnp.maximum(dist, 0)
    large = max_exact + (jnp.log(jnp.maximum(d, 1).astype(jnp.float32) / max_exact)
                         / math.log(REL_MAX_DIST / max_exact) * (REL_BUCKETS - max_exact)).astype(jnp.int32)
    large = jnp.minimum(large, REL_BUCKETS - 1)
    return jnp.where(d < max_exact, d, large)


def moba_attention(x, w_qkv, w_o, rel_bias):
    B, S, _ = x.shape
    H, Dh, L = ATTN_HEADS, ATTN_HEAD_DIM, MOBA_BLOCK
    nb = -(-S // L)
    k_top = min(MOBA_TOPK, nb)
    n_chunks = S // Q_CHUNK
    C = Q_CHUNK
    scale = Dh ** -0.5
    f32 = jnp.float32
    q, k, v = jnp.split(x @ w_qkv, 3, axis=-1)
    heads = lambda t: t.reshape(B, S, H, Dh).transpose(0, 2, 1, 3)
    q, k, v = heads(q), heads(k), heads(v)
    pad = ((0, 0), (0, 0), (0, nb * L - S), (0, 0))
    kb = jnp.pad(k, pad).reshape(B, H, nb, L, Dh)
    vb = jnp.pad(v, pad).reshape(B, H, nb, L, Dh)
    k_mean = jnp.mean(kb.astype(f32), axis=3)
    head_idx = jnp.arange(H)[:, None, None, None]

    def one_chunk(bc):
        b, c = bc
        t0 = c * C
        j = t0 // L
        q_c = lax.dynamic_slice_in_dim(q[b], t0, C, axis=1)
        kb_b, vb_b = kb[b], vb[b]
        pos_q = t0 + jnp.arange(C)
        gate = jnp.einsum("hqd,hnd->hqn", q_c.astype(f32), k_mean[b])
        gate = jnp.where(jnp.arange(nb) < j, gate, -jnp.inf)
        _, sel = lax.top_k(gate, k_top)
        valid = jnp.arange(k_top) < j
        k_sel = jax.vmap(lambda blk, s: blk[s])(kb_b, sel)
        v_sel = jax.vmap(lambda blk, s: blk[s])(vb_b, sel)
        s_sel = jnp.einsum("hqd,hqnld->hqnl", q_c, k_sel, preferred_element_type=f32) * scale
        pos_sel = sel[..., None] * L + jnp.arange(L)
        b_sel = rel_bias[head_idx, rel_bucket(pos_q[None, :, None, None] - pos_sel)].astype(f32)
        s_sel = jnp.where(valid[None, None, :, None], s_sel + b_sel, -jnp.inf)
        k_own = lax.dynamic_index_in_dim(kb_b, j, axis=1, keepdims=False)
        v_own = lax.dynamic_index_in_dim(vb_b, j, axis=1, keepdims=False)
        dist_own = pos_q[:, None] - (j * L + jnp.arange(L))[None, :]
        s_own = (jnp.einsum("hqd,hld->hql", q_c, k_own, preferred_element_type=f32) * scale
                 + rel_bias[:, rel_bucket(dist_own)].astype(f32))
        s_own = jnp.where(dist_own[None] >= 0, s_own, -jnp.inf)
        p = jax.nn.softmax(jnp.concatenate([s_own, s_sel.reshape(H, C, k_top * L)], axis=-1), axis=-1)
        p_own = p[..., :L].astype(v.dtype)
        p_sel = p[..., L:].reshape(H, C, k_top, L).astype(v.dtype)
        o = (jnp.einsum("hql,hld->hqd", p_own, v_own, preferred_element_type=f32)
             + jnp.einsum("hqnl,hqnld->hqd", p_sel, v_sel, preferred_element_type=f32))
        return o.transpose(1, 0, 2).reshape(C, H * Dh).astype(x.dtype)

    b_idx = jnp.repeat(jnp.arange(B), n_chunks)
    c_idx = jnp.tile(jnp.arange(n_chunks), B)
    o = lax.map(one_chunk, (b_idx, c_idx)).reshape(B, S, H * Dh)
    return o @ w_o


def chunk_gated_delta_rule(q, k, v, g, beta):
    B, S, H, Dk = q.shape
    Dv = v.shape[-1]
    C = GDN_CHUNK
    N = S // C
    chunks = lambda t: t.reshape(B, N, C, H, t.shape[-1]).transpose(0, 3, 1, 2, 4)
    q, k, v = chunks(q), chunks(k), chunks(v)
    g = g.reshape(B, N, C, H).transpose(0, 3, 1, 2)
    beta = beta.reshape(B, N, C, H).transpose(0, 3, 1, 2)
    G = jnp.cumsum(g, axis=-1)
    idx = jnp.arange(C)
    causal = idx[:, None] >= idx[None, :]
    strict = idx[:, None] > idx[None, :]
    decay = jnp.exp(jnp.where(causal, G[..., :, None] - G[..., None, :], -jnp.inf))
    kk = jnp.einsum("bhncd,bhnsd->bhncs", k, k)
    low = jnp.where(strict, beta[..., None] * kk * decay, 0.0)
    eye = jnp.eye(C, dtype=q.dtype)
    rhs = jnp.concatenate([v * beta[..., None], k * (beta * jnp.exp(G))[..., None]], axis=-1)
    sol = lax.linalg.triangular_solve(eye + low, rhs, left_side=True, lower=True, unit_diagonal=True)
    u, w = sol[..., :Dv], sol[..., Dv:]
    attn = jnp.where(causal, jnp.einsum("bhncd,bhnsd->bhncs", q, k) * decay, 0.0)
    q_dec = q * jnp.exp(G)[..., None]
    k_dec = k * jnp.exp(G[..., -1:] - G)[..., None]
    g_last = jnp.exp(G[..., -1])

    def step(state, inp):
        u_i, w_i, qd_i, kd_i, a_i, gl_i = inp
        v_new = u_i - jnp.einsum("bhcd,bhde->bhce", w_i, state)
        out = jnp.einsum("bhcd,bhde->bhce", qd_i, state) + jnp.einsum("bhcs,bhse->bhce", a_i, v_new)
        state = state * gl_i[..., None, None] + jnp.einsum("bhcd,bhce->bhde", kd_i, v_new)
        return state, out

    xs = tuple(jnp.moveaxis(t, 2, 0) for t in (u, w, q_dec, k_dec, attn, g_last))
    state0 = jnp.zeros((B, H, Dk, Dv), jnp.float32)
    _, out = lax.scan(step, state0, xs)
    return out.transpose(1, 0, 3, 2, 4).reshape(B, S, H, Dv)


def gated_deltanet(x, w_in, conv_w, a_log, dt_bias, o_norm, w_o):
    B, S, _ = x.shape
    H, Dk = GDN_HEADS, GDN_HEAD_DIM
    Dv = Dk
    f32 = jnp.float32
    proj = x @ w_in
    n_qkv = 2 * H * Dk + H * Dv
    qkv = jax.nn.silu(causal_dwconv(proj[..., :n_qkv], conv_w)).astype(f32)
    z = proj[..., n_qkv:n_qkv + H * Dv].reshape(B, S, H, Dv).astype(f32)
    b_in = proj[..., n_qkv + H * Dv:n_qkv + H * Dv + H].astype(f32)
    a_in = proj[..., n_qkv + H * Dv + H:].astype(f32)
    q = qkv[..., :H * Dk].reshape(B, S, H, Dk)
    k = qkv[..., H * Dk:2 * H * Dk].reshape(B, S, H, Dk)
    v = qkv[..., 2 * H * Dk:].reshape(B, S, H, Dv)
    q = l2norm(q) * (Dk ** -0.5)
    k = l2norm(k)
    beta = jax.nn.sigmoid(b_in)
    g = -jnp.exp(a_log.astype(f32)) * jax.nn.softplus(a_in + dt_bias.astype(f32))
    o = chunk_gated_delta_rule(q, k, v, g, beta)
    o = rmsnorm(o, o_norm) * jax.nn.silu(z)
    return o.reshape(B, S, H * Dv).astype(x.dtype) @ w_o


def conv_glu_ffn(x, w_up, conv_w, conv_b, w_down):
    gate, val = jnp.split(x @ w_up, 2, axis=-1)
    gate = causal_dwconv(gate, conv_w) + conv_b
    return (jax.nn.silu(gate) * val) @ w_down


def setup_inputs(seed: int = 0) -> dict:
    key = jax.random.key(seed)
    ks = iter(jax.random.split(key, 32))
    f32 = jnp.float32
    D = D_MODEL
    n_attn = (DEPTH + N_MIXERS - 1) // N_MIXERS
    n_gdn = DEPTH // N_MIXERS
    gdn_in = 3 * GDN_HEADS * GDN_HEAD_DIM + GDN_HEADS * GDN_HEAD_DIM + 2 * GDN_HEADS

    def w(shape, fan_in):
        return jax.random.normal(next(ks), shape, f32) * fan_in ** -0.5

    def gain(shape):
        return 1.0 + 0.05 * jax.random.normal(next(ks), shape, f32)

    x = jax.random.normal(next(ks), (BATCH, SEQ, D), f32)
    rel_bias = 0.5 * jax.random.normal(next(ks), (ATTN_HEADS, REL_BUCKETS), f32)
    attn_norm = gain((n_attn, D))
    attn_w_qkv = w((n_attn, D, 3 * ATTN_HEADS * ATTN_HEAD_DIM), D)
    attn_w_o = w((n_attn, ATTN_HEADS * ATTN_HEAD_DIM, D), ATTN_HEADS * ATTN_HEAD_DIM)
    gdn_norm = gain((n_gdn, D))
    gdn_w_in = w((n_gdn, D, gdn_in), D)
    gdn_conv_w = w((n_gdn, GDN_CONV, 3 * GDN_HEADS * GDN_HEAD_DIM), GDN_CONV)
    gdn_a_log = jnp.log(jax.random.uniform(next(ks), (n_gdn, GDN_HEADS), f32, 1.0, 16.0))
    dt = jnp.exp(jax.random.uniform(next(ks), (n_gdn, GDN_HEADS), f32, math.log(1e-3), math.log(1e-1)))
    gdn_dt_bias = dt + jnp.log(-jnp.expm1(-dt))
    gdn_o_norm = gain((n_gdn, GDN_HEAD_DIM))
    gdn_w_o = w((n_gdn, GDN_HEADS * GDN_HEAD_DIM, D), GDN_HEADS * GDN_HEAD_DIM)
    ffn_norm = gain((DEPTH, D))
    ffn_w_up = w((DEPTH, D, 2 * D_FF), D)
    ffn_conv_w = w((DEPTH, FFN_CONV, D_FF), FFN_CONV)
    ffn_conv_b = 0.02 * jax.random.normal(next(ks), (DEPTH, D_FF), f32)
    ffn_w_down = w((DEPTH, D_FF, D), D_FF)
    final_norm = gain((D,))
    return {"x": x, "rel_bias": rel_bias,
            "attn_norm": attn_norm, "attn_w_qkv": attn_w_qkv, "attn_w_o": attn_w_o,
            "gdn_norm": gdn_norm, "gdn_w_in": gdn_w_in, "gdn_conv_w": gdn_conv_w,
            "gdn_a_log": gdn_a_log, "gdn_dt_bias": gdn_dt_bias, "gdn_o_norm": gdn_o_norm,
            "gdn_w_o": gdn_w_o,
            "ffn_norm": ffn_norm, "ffn_w_up": ffn_w_up, "ffn_conv_w": ffn_conv_w,
            "ffn_conv_b": ffn_conv_b, "ffn_w_down": ffn_w_down, "final_norm": final_norm}


def reference(x, rel_bias, attn_norm, attn_w_qkv, attn_w_o, gdn_norm, gdn_w_in, gdn_conv_w,
              gdn_a_log, gdn_dt_bias, gdn_o_norm, gdn_w_o, ffn_norm, ffn_w_up, ffn_conv_w,
              ffn_conv_b, ffn_w_down, final_norm):
    h = x
    for i in range(DEPTH):
        m = i // N_MIXERS
        if i % N_MIXERS == 0:
            h = h + moba_attention(rmsnorm(h, attn_norm[m]), attn_w_qkv[m], attn_w_o[m], rel_bias)
        else:
            h = h + gated_deltanet(rmsnorm(h, gdn_norm[m]), gdn_w_in[m], gdn_conv_w[m], gdn_a_log[m],
                                   gdn_dt_bias[m], gdn_o_norm[m], gdn_w_o[m])
        h = h + conv_glu_ffn(rmsnorm(h, ffn_norm[i]), ffn_w_up[i], ffn_conv_w[i], ffn_conv_b[i], ffn_w_down[i])
    return rmsnorm(h, final_norm)
```

```python
import functools
import math

import jax
import jax.numpy as jnp
from jax import lax
from jax.experimental import pallas as pl
from jax.experimental.pallas import tpu as pltpu

F32 = jnp.float32
BF16 = jnp.bfloat16

D_MODEL = 1024
HEADS = 8
HEAD_DIM = 128
MOBA_BLOCK = 256
MOBA_TOPK = 3
REL_BUCKETS = 32
REL_MAX_DIST = 1024
GDN_CONV = 4
GDN_CHUNK = 128
D_FF = 2816
FFN_CONV = 3
NORM_EPS = 1e-6
L2_EPS = 1e-6

NEG = -1e30
LOG2E = 1.4426950408889634
N_BIAS_TILES = 6
VMEM_LIMIT = 56 * 1024 * 1024

_NT = (((1,), (1,)), ((), ()))


def _cparams(n_axes):
    return pltpu.CompilerParams(dimension_semantics=("arbitrary",) * n_axes,
                                vmem_limit_bytes=VMEM_LIMIT)


def _rms(x, g):
    return x * lax.rsqrt(jnp.mean(x * x, axis=-1, keepdims=True) + NORM_EPS) * g


def _qkv_kernel(x_ref, g_ref, wqk_ref, wvt_ref, qk_ref, vt_ref):
    xn = _rms(x_ref[...], g_ref[...]).astype(BF16)
    tm = xn.shape[0]
    n_qk = wqk_ref.shape[1]
    for c in range(0, n_qk, 512):
        qk_ref[:, c:c + 512] = jnp.dot(xn, wqk_ref[:, c:c + 512],
                                       preferred_element_type=F32).astype(BF16)
    for h in range(HEADS):
        vt = lax.dot_general(wvt_ref[h * HEAD_DIM:(h + 1) * HEAD_DIM, :], xn, _NT,
                             preferred_element_type=F32)
        for r in range(tm // MOBA_BLOCK):
            vt_ref[0, h, r] = vt[:, r * MOBA_BLOCK:(r + 1) * MOBA_BLOCK].astype(BF16)


def _qkv_proj(x2d, g, w_qk, w_vt, batch, seq, tm=512):
    m = x2d.shape[0]
    nb = seq // MOBA_BLOCK
    tps = seq // tm
    rb = tm // MOBA_BLOCK
    return pl.pallas_call(
        _qkv_kernel,
        grid=(m // tm,),
        in_specs=[
            pl.BlockSpec((tm, D_MODEL), lambda i: (i, 0)),
            pl.BlockSpec((1, D_MODEL), lambda i: (0, 0)),
            pl.BlockSpec(w_qk.shape, lambda i: (0, 0)),
            pl.BlockSpec(w_vt.shape, lambda i: (0, 0)),
        ],
        out_specs=[
            pl.BlockSpec((tm, w_qk.shape[1]), lambda i: (i, 0)),
            pl.BlockSpec((1, HEADS, rb, HEAD_DIM, MOBA_BLOCK), lambda i: (i // tps, 0, i % tps, 0, 0)),
        ],
        out_shape=[
            jax.ShapeDtypeStruct((m, w_qk.shape[1]), BF16),
            jax.ShapeDtypeStruct((batch, HEADS, nb, HEAD_DIM, MOBA_BLOCK), BF16),
        ],
        compiler_params=_cparams(1),
        name="qkv_proj",
    )(x2d, g, w_qk, w_vt)


def _moba_kernel(q_ref, k_ref, vt_ref, bias_ref, o_ref, kmean_ref, sel_ref, *, nb):
    j = pl.program_id(2)
    blk = MOBA_BLOCK

    @pl.when(j == 0)
    def _():
        kf = k_ref[0].astype(F32).reshape(nb, blk, HEAD_DIM)
        kmean_ref[...] = jnp.sum(kf, axis=1) * (1.0 / blk)

    q = q_ref[0]
    gate = lax.dot_general(kmean_ref[...], q.astype(F32), _NT,
                           precision=lax.Precision.HIGHEST, preferred_element_type=F32)
    row = lax.broadcasted_iota(jnp.int32, gate.shape, 0)
    g = jnp.where(row < j, gate, NEG)
    sel = jnp.zeros(gate.shape, F32)
    for _ in range(MOBA_TOPK):
        mx = jnp.max(g, axis=0, keepdims=True)
        idx = jnp.min(jnp.where(g == mx, row, nb), axis=0, keepdims=True)
        pick = (row == idx) & (mx > 0.5 * NEG)
        sel = jnp.where(pick, 1.0, sel)
        g = jnp.where(pick, NEG, g)
    sel_ref[...] = sel

    qs = (q.astype(F32) * (HEAD_DIM ** -0.5 * LOG2E)).astype(BF16)

    def block(n, dlt, carry, selrow):
        m, l, acc = carry
        kn = k_ref[0, pl.ds(pl.multiple_of(n * blk, blk), blk), :]
        s = lax.dot_general(kn, qs, _NT, preferred_element_type=F32)
        s = s + bias_ref[0, dlt]
        if selrow is not None:
            s = jnp.where(selrow > 0.0, s, NEG)
        m_new = jnp.maximum(m, jnp.max(s, axis=0, keepdims=True))
        alpha = jnp.exp2(m - m_new)
        p = jnp.exp2(s - m_new)
        l = alpha * l + jnp.sum(p, axis=0, keepdims=True)
        acc = alpha * acc + jnp.dot(vt_ref[0, 0, n], p.astype(BF16), preferred_element_type=F32)
        return m_new, l, acc

    init = (jnp.full((1, blk), NEG, F32), jnp.zeros((1, blk), F32), jnp.zeros((HEAD_DIM, blk), F32))
    carry = block(j, 0, init, None)

    def body(n, carry):
        dlt = jnp.minimum(j - n, N_BIAS_TILES - 1)
        return block(n, dlt, carry, sel_ref[pl.ds(n, 1), :])

    _, l, acc = lax.fori_loop(0, j, body, carry)
    o_ref[0] = (acc / l).T.astype(o_ref.dtype)


def _moba_attention(qk3, vt, bias_tiles, batch, seq):
    nb = seq // MOBA_BLOCK
    return pl.pallas_call(
        functools.partial(_moba_kernel, nb=nb),
        grid=(batch, HEADS, nb),
        in_specs=[
            pl.BlockSpec((1, MOBA_BLOCK, HEAD_DIM), lambda b, h, j: (b, j, h)),
            pl.BlockSpec((1, seq, HEAD_DIM), lambda b, h, j: (b, 0, HEADS + h)),
            pl.BlockSpec((1, 1, nb, HEAD_DIM, MOBA_BLOCK), lambda b, h, j: (b, h, 0, 0, 0)),
            pl.BlockSpec((1, N_BIAS_TILES, MOBA_BLOCK, MOBA_BLOCK), lambda b, h, j: (h, 0, 0, 0)),
        ],
        out_specs=pl.BlockSpec((1, MOBA_BLOCK, HEAD_DIM), lambda b, h, j: (b, j, h)),
        out_shape=jax.ShapeDtypeStruct((batch, seq, HEADS * HEAD_DIM), BF16),
        scratch_shapes=[pltpu.VMEM((nb, HEAD_DIM), F32), pltpu.VMEM((nb, MOBA_BLOCK), F32)],
        compiler_params=_cparams(3),
        name="moba_attention",
    )(qk3, qk3, vt, bias_tiles)


def _rel_bias_tiles(rel_bias):
    max_exact = REL_BUCKETS // 2
    ki = jnp.arange(MOBA_BLOCK)[:, None]
    qi = jnp.arange(MOBA_BLOCK)[None, :]
    dlt = jnp.arange(N_BIAS_TILES)[:, None, None]
    dist = dlt * MOBA_BLOCK + qi - ki
    d = jnp.maximum(dist, 0)
    large = max_exact + (jnp.log(jnp.maximum(d, 1).astype(F32) / max_exact)
                         / math.log(REL_MAX_DIST / max_exact) * (REL_BUCKETS - max_exact)).astype(jnp.int32)
    bucket = jnp.where(d < max_exact, d, jnp.minimum(large, REL_BUCKETS - 1))
    tiles = rel_bias.astype(F32)[:, bucket] * LOG2E
    return jnp.where(dist[None] >= 0, tiles, NEG)


def _proj_res_kernel(a_ref, w_ref, r_ref, o_ref):
    a = a_ref[...]
    for c in range(0, w_ref.shape[1], 512):
        o_ref[:, c:c + 512] = r_ref[:, c:c + 512] + jnp.dot(a, w_ref[:, c:c + 512],
                                                           preferred_element_type=F32)


def _proj_residual(a, w, res, tm=512):
    m, k = a.shape
    n = w.shape[1]
    return pl.pallas_call(
        _proj_res_kernel,
        grid=(m // tm,),
        in_specs=[
            pl.BlockSpec((tm, k), lambda i: (i, 0)),
            pl.BlockSpec((k, n), lambda i: (0, 0)),
            pl.BlockSpec((tm, n), lambda i: (i, 0)),
        ],
        out_specs=pl.BlockSpec((tm, n), lambda i: (i, 0)),
        out_shape=jax.ShapeDtypeStruct((m, n), F32),
        compiler_params=_cparams(1),
        name="proj_residual",
    )(a, w, res)


def _ffn_kernel(x_ref, g_ref, wg_ref, wv_ref, cw_ref, cb_ref, wd_ref, fg_ref, o_ref,
                xn_ref, acc_ref, gbuf_ref, carry_ref, *, tiles_per_seq, final_norm):
    i = pl.program_id(0)
    f = pl.program_id(1)
    nf = pl.num_programs(1)
    tm = x_ref.shape[0]

    @pl.when(f == 0)
    def _():
        xn_ref[...] = _rms(x_ref[...], g_ref[...]).astype(BF16)
        acc_ref[...] = jnp.zeros_like(acc_ref)

    xn = xn_ref[...]
    gate = jnp.dot(xn, wg_ref[...], preferred_element_type=F32)
    val = jnp.dot(xn, wv_ref[...], preferred_element_type=F32)

    @pl.when(i % tiles_per_seq == 0)
    def _():
        carry_ref[f] = jnp.zeros(carry_ref.shape[1:], F32)

    gbuf_ref[0:8, :] = carry_ref[f]
    gbuf_ref[8:8 + tm, :] = gate
    carry_ref[f] = gate[tm - 8:tm, :]
    cw = cw_ref[0]
    y = cb_ref[0] + cw[2:3, :] * gate
    y = y + cw[1:2, :] * gbuf_ref[7:7 + tm, :]
    y = y + cw[0:1, :] * gbuf_ref[6:6 + tm, :]
    act = (y * jax.nn.sigmoid(y) * val).astype(BF16)
    acc_ref[...] += jnp.dot(act, wd_ref[...], preferred_element_type=F32)

    @pl.when(f == nf - 1)
    def _():
        h = x_ref[...] + acc_ref[...]
        if final_norm:
            h = _rms(h, fg_ref[...])
        o_ref[...] = h


def _conv_glu_ffn(x2d, g, w_gate, w_val, conv_w, conv_b, w_down, final_g, seq, final_norm, tm=1024, tf=256):
    m = x2d.shape[0]
    nf = D_FF // tf
    cw = conv_w.reshape(FFN_CONV, nf, tf).transpose(1, 0, 2)
    cb = conv_b.reshape(nf, 1, tf)
    return pl.pallas_call(
        functools.partial(_ffn_kernel, tiles_per_seq=seq // tm, final_norm=final_norm),
        grid=(m // tm, nf),
        in_specs=[
            pl.BlockSpec((tm, D_MODEL), lambda i, f: (i, 0)),
            pl.BlockSpec((1, D_MODEL), lambda i, f: (0, 0)),
            pl.BlockSpec((D_MODEL, tf), lambda i, f: (0, f)),
            pl.BlockSpec((D_MODEL, tf), lambda i, f: (0, f)),
            pl.BlockSpec((1, FFN_CONV, tf), lambda i, f: (f, 0, 0)),
            pl.BlockSpec((1, 1, tf), lambda i, f: (f, 0, 0)),
            pl.BlockSpec((tf, D_MODEL), lambda i, f: (f, 0)),
            pl.BlockSpec((1, D_MODEL), lambda i, f: (0, 0)),
        ],
        out_specs=pl.BlockSpec((tm, D_MODEL), lambda i, f: (i, 0)),
        out_shape=jax.ShapeDtypeStruct((m, D_MODEL), F32),
        scratch_shapes=[
            pltpu.VMEM((tm, D_MODEL), BF16),
            pltpu.VMEM((tm, D_MODEL), F32),
            pltpu.VMEM((tm + 8, tf), F32),
            pltpu.VMEM((nf, 8, tf), F32),
        ],
        compiler_params=_cparams(2),
        name="conv_glu_ffn",
    )(x2d, g, w_gate, w_val, cw, cb, w_down, final_g)


def _gdn_inproj_kernel(x_ref, g_ref, w_ref, wab_ref, wabt_ref, o_ref, ab_ref, abt_ref, xn_ref):
    @pl.when(pl.program_id(1) == 0)
    def _():
        xn = _rms(x_ref[...], g_ref[...])
        xn_ref[...] = xn.astype(BF16)
        ab_ref[...] = jnp.dot(xn, wab_ref[...], precision=lax.Precision.HIGHEST,
                              preferred_element_type=F32)
        abt_ref[...] = lax.dot_general(wabt_ref[...], xn, _NT, precision=lax.Precision.HIGHEST,
                                       preferred_element_type=F32)

    o_ref[...] = jnp.dot(xn_ref[...], w_ref[...], preferred_element_type=F32).astype(BF16)


def _gdn_inproj(x2d, g, w_main, w_ab, w_abt, tm=512, tn=1024):
    m = x2d.shape[0]
    n = w_main.shape[1]
    return pl.pallas_call(
        _gdn_inproj_kernel,
        grid=(m // tm, n // tn),
        in_specs=[
            pl.BlockSpec((tm, D_MODEL), lambda i, j: (i, 0)),
            pl.BlockSpec((1, D_MODEL), lambda i, j: (0, 0)),
            pl.BlockSpec((D_MODEL, tn), lambda i, j: (0, j)),
            pl.BlockSpec(w_ab.shape, lambda i, j: (0, 0)),
            pl.BlockSpec(w_abt.shape, lambda i, j: (0, 0)),
        ],
        out_specs=[
            pl.BlockSpec((tm, tn), lambda i, j: (i, j)),
            pl.BlockSpec((tm, w_ab.shape[1]), lambda i, j: (i, 0)),
            pl.BlockSpec((w_abt.shape[0], tm), lambda i, j: (0, i)),
        ],
        out_shape=[
            jax.ShapeDtypeStruct((m, n), BF16),
            jax.ShapeDtypeStruct((m, w_ab.shape[1]), F32),
            jax.ShapeDtypeStruct((w_abt.shape[0], m), F32),
        ],
        scratch_shapes=[pltpu.VMEM((tm, D_MODEL), BF16)],
        compiler_params=_cparams(2),
        name="gdn_inproj",
    )(x2d, g, w_main, w_ab, w_abt)


def _softplus(x):
    return jnp.maximum(x, 0.0) + jnp.log1p(jnp.exp(-jnp.abs(x)))


def _chunk_cumsum(x, axis):
    pos = lax.broadcasted_iota(jnp.int32, x.shape, axis) % GDN_CHUNK
    s = 1
    while s < GDN_CHUNK:
        x = x + jnp.where(pos >= s, pltpu.roll(x, s, axis), 0.0)
        s *= 2
    return x


def _gdn_prep_kernel(p_ref, cw_ref, ab_ref, abt_ref, alog_r_ref, dt_r_ref, alog_c_ref, dt_c_ref,
                     q_ref, k_ref, v_ref, gc_ref, bc_ref, gt_ref, xbuf_ref, *, tiles_per_seq):
    i = pl.program_id(0)
    tm = p_ref.shape[0]
    nh = HEADS * HEAD_DIM

    @pl.when(i % tiles_per_seq == 0)
    def _():
        xbuf_ref[0:8, :] = jnp.zeros((8, xbuf_ref.shape[1]), F32)

    xbuf_ref[8:8 + tm, :] = p_ref[...].astype(F32)
    outs = (q_ref, k_ref, v_ref)
    for s in range(3 * HEADS):
        c0 = s * HEAD_DIM
        cw = cw_ref[:, c0:c0 + HEAD_DIM]
        y = cw[3:4, :] * xbuf_ref[8:8 + tm, c0:c0 + HEAD_DIM]
        y = y + cw[2:3, :] * xbuf_ref[7:7 + tm, c0:c0 + HEAD_DIM]
        y = y + cw[1:2, :] * xbuf_ref[6:6 + tm, c0:c0 + HEAD_DIM]
        y = y + cw[0:1, :] * xbuf_ref[5:5 + tm, c0:c0 + HEAD_DIM]
        y = y * jax.nn.sigmoid(y)
        which, h = divmod(s, HEADS)
        if which < 2:
            y = y * lax.rsqrt(jnp.sum(y * y, axis=-1, keepdims=True) + L2_EPS)
            if which == 0:
                y = y * (HEAD_DIM ** -0.5)
        outs[which][:, h * HEAD_DIM:(h + 1) * HEAD_DIM] = y.astype(BF16)
    xbuf_ref[0:8, :] = xbuf_ref[tm:tm + 8, :]

    ab = ab_ref[...]
    bc_ref[...] = jax.nn.sigmoid(ab[:, :128])
    g_col = -jnp.exp(alog_r_ref[...]) * _softplus(ab[:, 128:] + dt_r_ref[...])
    gc_ref[...] = _chunk_cumsum(g_col, 0)
    abt = abt_ref[...]
    g_row = -jnp.exp(alog_c_ref[...]) * _softplus(abt[HEADS:2 * HEADS, :] + dt_c_ref[...])
    g_row = _chunk_cumsum(g_row, 1)
    for r in range(tm // GDN_CHUNK):
        gt_ref[r] = g_row[:, r * GDN_CHUNK:(r + 1) * GDN_CHUNK]


def _gdn_prep(proj, conv_w, ab, abt, alog_r, dt_r, alog_c, dt_c, seq, tm=512):
    m = proj.shape[0]
    nh = HEADS * HEAD_DIM
    tok = lambda i: (i, 0)
    const = lambda i: (0, 0)
    return pl.pallas_call(
        functools.partial(_gdn_prep_kernel, tiles_per_seq=seq // tm),
        grid=(m // tm,),
        in_specs=[
            pl.BlockSpec((tm, 3 * nh), tok),
            pl.BlockSpec(conv_w.shape, const),
            pl.BlockSpec((tm, ab.shape[1]), tok),
            pl.BlockSpec((abt.shape[0], tm), lambda i: (0, i)),
            pl.BlockSpec(alog_r.shape, const),
            pl.BlockSpec(dt_r.shape, const),
            pl.BlockSpec(alog_c.shape, const),
            pl.BlockSpec(dt_c.shape, const),
        ],
        out_specs=[
            pl.BlockSpec((tm, nh), tok),
            pl.BlockSpec((tm, nh), tok),
            pl.BlockSpec((tm, nh), tok),
            pl.BlockSpec((tm, 128), tok),
            pl.BlockSpec((tm, 128), tok),
            pl.BlockSpec((tm // GDN_CHUNK, HEADS, GDN_CHUNK), lambda i: (i, 0, 0)),
        ],
        out_shape=[
            jax.ShapeDtypeStruct((m, nh), BF16),
            jax.ShapeDtypeStruct((m, nh), BF16),
            jax.ShapeDtypeStruct((m, nh), BF16),
            jax.ShapeDtypeStruct((m, 128), F32),
            jax.ShapeDtypeStruct((m, 128), F32),
            jax.ShapeDtypeStruct((m // GDN_CHUNK, HEADS, GDN_CHUNK), F32),
        ],
        scratch_shapes=[pltpu.VMEM((tm + 8, 3 * nh), F32)],
        compiler_params=_cparams(1),
        name="gdn_prep",
    )(proj, conv_w, ab, abt, alog_r, dt_r, alog_c, dt_c)


def _unit_lower_inverse(low):
    n = low.shape[0]
    ri = lax.broadcasted_iota(jnp.int32, (n, n), 0)
    ci = lax.broadcasted_iota(jnp.int32, (n, n), 1)
    t = jnp.where(ri == ci, 1.0, 0.0) - jnp.where((ri // 2 == ci // 2) & (ri % 2 == 1) & (ci % 2 == 0), low, 0.0)
    s = 2
    while s < n:
        off = (ri // (2 * s) == ci // (2 * s)) & ((ri // s) % 2 == 1) & ((ci // s) % 2 == 0)
        tb = t.astype(BF16)
        lt = jnp.dot(jnp.where(off, low, 0.0).astype(BF16), tb, preferred_element_type=F32)
        t = t - jnp.dot(tb, lt.astype(BF16), preferred_element_type=F32)
        s *= 2
    return t


def _gdn_chunk_kernel(q_ref, k_ref, v_ref, z_ref, gc_ref, bc_ref, gt_ref, on_ref, o_ref, s_ref):
    c = GDN_CHUNK
    n_chunks = q_ref.shape[0] // c

    @pl.when(pl.program_id(1) == 0)
    def _():
        s_ref[...] = jnp.zeros_like(s_ref)

    ri = lax.broadcasted_iota(jnp.int32, (c, c), 0)
    ci = lax.broadcasted_iota(jnp.int32, (c, c), 1)
    causal = ri >= ci
    strict = ri > ci
    onorm = on_ref[...]

    def chunk(ic, _):
        r0 = pl.multiple_of(ic * c, c)
        gc_t = gc_ref[pl.ds(r0, c), :]
        bc_t = bc_ref[pl.ds(r0, c), :]
        gt_t = gt_ref[ic]
        for h in range(HEADS):
            hs = slice(h * HEAD_DIM, (h + 1) * HEAD_DIM)
            q = q_ref[pl.ds(r0, c), hs]
            k = k_ref[pl.ds(r0, c), hs]
            kf = k.astype(F32)
            gcol = gc_t[:, h:h + 1]
            bcol = bc_t[:, h:h + 1]
            grow = gt_t[h:h + 1, :]
            glast = gcol[c - 1:c, :]
            egcol = jnp.exp(gcol)
            decay = jnp.exp(jnp.where(causal, gcol - grow, NEG))
            kq = lax.dot_general(jnp.concatenate([k, q], axis=0), k, _NT, preferred_element_type=F32)
            kk, qk = kq[:c], kq[c:]
            low = jnp.where(strict, bcol * kk * decay, 0.0)
            t = _unit_lower_inverse(low)
            rhs = jnp.concatenate([v_ref[pl.ds(r0, c), hs].astype(F32) * bcol,
                                   kf * (bcol * egcol)], axis=1).astype(BF16)
            sol = jnp.dot(t.astype(BF16), rhs, preferred_element_type=F32)
            u, w = sol[:, :HEAD_DIM], sol[:, HEAD_DIM:]
            attn = jnp.where(causal, qk * decay, 0.0)
            state = s_ref[h]
            wq = jnp.concatenate([w, q.astype(F32) * egcol], axis=0).astype(BF16)
            ws = jnp.dot(wq, state.astype(BF16), preferred_element_type=F32)
            v_new = u - ws[:c]
            kdt = (kf * jnp.exp(glast - gcol)).T
            ak = jnp.concatenate([attn, kdt], axis=0).astype(BF16)
            r2 = jnp.dot(ak, v_new.astype(BF16), preferred_element_type=F32)
            out = ws[c:] + r2[:c]
            s_ref[h] = state * jnp.exp(glast) + r2[c:]
            z = z_ref[pl.ds(r0, c), hs].astype(F32)
            o = _rms(out, onorm) * (z * jax.nn.sigmoid(z))
            o_ref[pl.ds(r0, c), hs] = o.astype(o_ref.dtype)
        return 0

    lax.fori_loop(0, n_chunks, chunk, 0)


def _gdn_chunks(q, k, v, proj, gc, bc, gt, o_norm, batch, seq, ts=1024):
    m = q.shape[0]
    nh = HEADS * HEAD_DIM
    tps = seq // ts
    tok = lambda b, t: (b * tps + t, 0)
    return pl.pallas_call(
        _gdn_chunk_kernel,
        grid=(batch, tps),
        in_specs=[
            pl.BlockSpec((ts, nh), tok),
            pl.BlockSpec((ts, nh), tok),
            pl.BlockSpec((ts, nh), tok),
            pl.BlockSpec((ts, nh), lambda b, t: (b * tps + t, 3)),
            pl.BlockSpec((ts, 128), tok),
            pl.BlockSpec((ts, 128), tok),
            pl.BlockSpec((ts // GDN_CHUNK, HEADS, GDN_CHUNK), lambda b, t: (b * tps + t, 0, 0)),
            pl.BlockSpec((1, HEAD_DIM), lambda b, t: (0, 0)),
        ],
        out_specs=pl.BlockSpec((ts, nh), tok),
        out_shape=jax.ShapeDtypeStruct((m, nh), BF16),
        scratch_shapes=[pltpu.VMEM((HEADS, HEAD_DIM, HEAD_DIM), F32)],
        compiler_params=_cparams(2),
        name="gdn_chunks",
    )(q, k, v, proj, gc, bc, gt, o_norm)


def _attention_layer(h2d, rel_bias, norm_g, w_qkv, w_o, batch, seq):
    nh = HEADS * HEAD_DIM
    w_qk = w_qkv[:, :2 * nh].astype(BF16)
    w_vt = w_qkv[:, 2 * nh:].T.astype(BF16)
    qk, vt = _qkv_proj(h2d, norm_g.reshape(1, -1), w_qk, w_vt, batch, seq)
    o = _moba_attention(qk.reshape(batch, seq, 2 * nh), vt, _rel_bias_tiles(rel_bias), batch, seq)
    return _proj_residual(o.reshape(batch * seq, nh), w_o.astype(BF16), h2d)


def _gdn_layer(h2d, norm_g, w_in, conv_w, a_log, dt_bias, o_norm, w_o, batch, seq):
    nh = HEADS * HEAD_DIM
    w_main = w_in[:, :4 * nh].astype(BF16)
    w_b = w_in[:, 4 * nh:4 * nh + HEADS]
    w_a = w_in[:, 4 * nh + HEADS:]
    pad = jnp.zeros((D_MODEL, 128 - HEADS), F32)
    w_ab = jnp.concatenate([w_b, pad, w_a, pad], axis=1)
    w_abt = jnp.concatenate([w_b, w_a], axis=1).T
    proj, ab, abt = _gdn_inproj(h2d, norm_g.reshape(1, -1), w_main, w_ab, w_abt)
    pad_r = jnp.zeros((128 - HEADS,), F32)
    alog_r = jnp.concatenate([a_log, pad_r]).reshape(1, 128)
    dt_r = jnp.concatenate([dt_bias, pad_r]).reshape(1, 128)
    q, k, v, gc, bc, gt = _gdn_prep(proj, conv_w, ab, abt, alog_r, dt_r,
                                    a_log.reshape(HEADS, 1), dt_bias.reshape(HEADS, 1), seq)
    o = _gdn_chunks(q, k, v, proj, gc, bc, gt, o_norm.reshape(1, -1), batch, seq)
    return _proj_residual(o, w_o.astype(BF16), h2d)


def _ffn_layer(h2d, norm_g, w_up, conv_w, conv_b, w_down, final_g, seq, final_norm):
    return _conv_glu_ffn(h2d, norm_g.reshape(1, -1), w_up[:, :D_FF].astype(BF16), w_up[:, D_FF:].astype(BF16),
                         conv_w, conv_b, w_down.astype(BF16), final_g.reshape(1, -1), seq, final_norm)


def kernel(x, rel_bias, attn_norm, attn_w_qkv, attn_w_o, gdn_norm, gdn_w_in, gdn_conv_w, gdn_a_log,
           gdn_dt_bias, gdn_o_norm, gdn_w_o, ffn_norm, ffn_w_up, ffn_conv_w, ffn_conv_b, ffn_w_down,
           final_norm):
    batch, seq, d = x.shape
    h = x.reshape(batch * seq, d)
    h = _attention_layer(h, rel_bias, attn_norm[0], attn_w_qkv[0], attn_w_o[0], batch, seq)
    h = _ffn_layer(h, ffn_norm[0], ffn_w_up[0], ffn_conv_w[0], ffn_conv_b[0], ffn_w_down[0],
                   final_norm, seq, False)
    h = _gdn_layer(h, gdn_norm[0], gdn_w_in[0], gdn_conv_w[0], gdn_a_log[0], gdn_dt_bias[0],
                   gdn_o_norm[0], gdn_w_o[0], batch, seq)
    h = _ffn_layer(h, ffn_norm[1], ffn_w_up[1], ffn_conv_w[1], ffn_conv_b[1], ffn_w_down[1],
                   final_norm, seq, True)
    return h.reshape(batch, seq, d)
```

```python
import functools
import math

import jax
import jax.numpy as jnp
from jax import lax
from jax.experimental import pallas as pl
from jax.experimental.pallas import tpu as pltpu

F32 = jnp.float32
BF16 = jnp.bfloat16

D_MODEL = 1024
HEADS = 8
HEAD_DIM = 128
MOBA_BLOCK = 256
MOBA_TOPK = 3
REL_BUCKETS = 32
REL_MAX_DIST = 1024
GDN_CONV = 4
GDN_CHUNK = 128
D_FF = 2816
FFN_CONV = 3
NORM_EPS = 1e-6
L2_EPS = 1e-6

NEG = -1e30
LOG2E = 1.4426950408889634
N_BIAS_TILES = 6
VMEM_LIMIT = 56 * 1024 * 1024

_NT = (((1,), (1,)), ((), ()))


def _cparams(n_axes):
    return pltpu.CompilerParams(dimension_semantics=("arbitrary",) * n_axes,
                                vmem_limit_bytes=VMEM_LIMIT)


def _rms(x, g):
    return x * lax.rsqrt(jnp.mean(x * x, axis=-1, keepdims=True) + NORM_EPS) * g


def _qkv_kernel(x_ref, g_ref, wqk_ref, wvt_ref, qk_ref, vt_ref):
    xn = _rms(x_ref[...], g_ref[...]).astype(BF16)
    tm = xn.shape[0]
    n_qk = wqk_ref.shape[1]
    for c in range(0, n_qk, 512):
        qk_ref[:, c:c + 512] = jnp.dot(xn, wqk_ref[:, c:c + 512],
                                       preferred_element_type=F32).astype(BF16)
    for h in range(HEADS):
        vt = lax.dot_general(wvt_ref[h * HEAD_DIM:(h + 1) * HEAD_DIM, :], xn, _NT,
                             preferred_element_type=F32)
        for r in range(tm // MOBA_BLOCK):
            vt_ref[0, h, r] = vt[:, r * MOBA_BLOCK:(r + 1) * MOBA_BLOCK].astype(BF16)


def _qkv_proj(x2d, g, w_qk, w_vt, batch, seq, tm=512):
    m = x2d.shape[0]
    nb = seq // MOBA_BLOCK
    tps = seq // tm
    rb = tm // MOBA_BLOCK
    return pl.pallas_call(
        _qkv_kernel,
        grid=(m // tm,),
        in_specs=[
            pl.BlockSpec((tm, D_MODEL), lambda i: (i, 0)),
            pl.BlockSpec((1, D_MODEL), lambda i: (0, 0)),
            pl.BlockSpec(w_qk.shape, lambda i: (0, 0)),
            pl.BlockSpec(w_vt.shape, lambda i: (0, 0)),
        ],
        out_specs=[
            pl.BlockSpec((tm, w_qk.shape[1]), lambda i: (i, 0)),
            pl.BlockSpec((1, HEADS, rb, HEAD_DIM, MOBA_BLOCK), lambda i: (i // tps, 0, i % tps, 0, 0)),
        ],
        out_shape=[
            jax.ShapeDtypeStruct((m, w_qk.shape[1]), BF16),
            jax.ShapeDtypeStruct((batch, HEADS, nb, HEAD_DIM, MOBA_BLOCK), BF16),
        ],
        compiler_params=_cparams(1),
        name="qkv_proj",
    )(x2d, g, w_qk, w_vt)


def _moba_kernel(q_ref, k_ref, vt_ref, bias_ref, o_ref, kmean_ref, sel_ref, s_ref, *, nb, unroll):
    j = pl.program_id(2)
    blk = MOBA_BLOCK

    @pl.when(j == 0)
    def _():
        kf = k_ref[0].astype(F32).reshape(nb, blk, HEAD_DIM)
        kmean_ref[...] = jnp.sum(kf, axis=1) * (1.0 / blk)

    q = q_ref[0]
    gate = lax.dot_general(kmean_ref[...], q.astype(F32), _NT,
                           precision=lax.Precision.HIGHEST, preferred_element_type=F32)
    row = lax.broadcasted_iota(jnp.int32, gate.shape, 0)
    g = jnp.where(row < j, gate, NEG)
    sel = jnp.zeros(gate.shape, F32)
    for _ in range(MOBA_TOPK):
        mx = jnp.max(g, axis=0, keepdims=True)
        idx = jnp.min(jnp.where(g == mx, row, nb), axis=0, keepdims=True)
        pick = (row == idx) & (mx > 0.5 * NEG)
        sel = jnp.where(pick, 1.0, sel)
        g = jnp.where(pick, NEG, g)
    sel_ref[...] = sel

    qs = (q.astype(F32) * (HEAD_DIM ** -0.5 * LOG2E)).astype(BF16)

    def scores(n, dlt, selrow):
        kn = k_ref[0, pl.ds(pl.multiple_of(n * blk, blk), blk), :]
        s = lax.dot_general(kn, qs, _NT, preferred_element_type=F32)
        s = s + bias_ref[0, dlt]
        if selrow is not None:
            s = jnp.where(selrow > 0.0, s, NEG)
        return s

    def fold8(x, op):
        return op(x.reshape(blk // 8, 8, blk), axis=0)

    s_own = scores(j, 0, None)
    s_ref[nb + unroll] = s_own
    n_groups = (j + unroll - 1) // unroll
    jm1 = jnp.maximum(j - 1, 0)

    def pass_a(g, m8):
        for i in range(unroll):
            n = g * unroll + i
            nc = jnp.minimum(n, jm1)
            selrow = jnp.where(n < j, sel_ref[pl.ds(nc, 1), :], 0.0)
            s = scores(nc, jnp.minimum(j - nc, N_BIAS_TILES - 1), selrow)
            s_ref[n] = s
            m8 = jnp.maximum(m8, fold8(s, jnp.max))
        return m8

    m8 = lax.fori_loop(0, n_groups, pass_a, fold8(s_own, jnp.max))
    m = jnp.max(m8, axis=0, keepdims=True)

    p = jnp.exp2(s_ref[nb + unroll] - m)
    l8 = fold8(p, jnp.sum)
    acc = jnp.dot(vt_ref[0, 0, j], p.astype(BF16), preferred_element_type=F32)

    def pass_b(g, carry):
        l8, acc = carry
        for i in range(unroll):
            n = g * unroll + i
            nc = jnp.minimum(n, jm1)
            p = jnp.exp2(s_ref[n] - m)
            l8 = l8 + fold8(p, jnp.sum)
            acc = acc + jnp.dot(vt_ref[0, 0, nc], p.astype(BF16), preferred_element_type=F32)
        return l8, acc

    l8, acc = lax.fori_loop(0, n_groups, pass_b, (l8, acc))
    l = jnp.sum(l8, axis=0, keepdims=True)
    o_ref[0] = (acc / l).T.astype(o_ref.dtype)


def _moba_attention(qk3, vt, bias_tiles, batch, seq, unroll=4):
    nb = seq // MOBA_BLOCK
    return pl.pallas_call(
        functools.partial(_moba_kernel, nb=nb, unroll=unroll),
        grid=(batch, HEADS, nb),
        in_specs=[
            pl.BlockSpec((1, MOBA_BLOCK, HEAD_DIM), lambda b, h, j: (b, j, h)),
            pl.BlockSpec((1, seq, HEAD_DIM), lambda b, h, j: (b, 0, HEADS + h)),
            pl.BlockSpec((1, 1, nb, HEAD_DIM, MOBA_BLOCK), lambda b, h, j: (b, h, 0, 0, 0)),
            pl.BlockSpec((1, N_BIAS_TILES, MOBA_BLOCK, MOBA_BLOCK), lambda b, h, j: (h, 0, 0, 0)),
        ],
        out_specs=pl.BlockSpec((1, MOBA_BLOCK, HEAD_DIM), lambda b, h, j: (b, j, h)),
        out_shape=jax.ShapeDtypeStruct((batch, seq, HEADS * HEAD_DIM), BF16),
        scratch_shapes=[pltpu.VMEM((nb, HEAD_DIM), F32), pltpu.VMEM((nb, MOBA_BLOCK), F32),
                        pltpu.VMEM((nb + unroll + 1, MOBA_BLOCK, MOBA_BLOCK), F32)],
        compiler_params=_cparams(3),
        name="moba_attention",
    )(qk3, qk3, vt, bias_tiles)


def _rel_bias_tiles(rel_bias):
    max_exact = REL_BUCKETS // 2
    ki = jnp.arange(MOBA_BLOCK)[:, None]
    qi = jnp.arange(MOBA_BLOCK)[None, :]
    dlt = jnp.arange(N_BIAS_TILES)[:, None, None]
    dist = dlt * MOBA_BLOCK + qi - ki
    d = jnp.maximum(dist, 0)
    large = max_exact + (jnp.log(jnp.maximum(d, 1).astype(F32) / max_exact)
                         / math.log(REL_MAX_DIST / max_exact) * (REL_BUCKETS - max_exact)).astype(jnp.int32)
    bucket = jnp.where(d < max_exact, d, jnp.minimum(large, REL_BUCKETS - 1))
    onehot = (bucket[None] == jnp.arange(REL_BUCKETS)[:, None, None, None]).astype(F32)
    tiles = jnp.einsum("hb,btkq->htkq", rel_bias.astype(F32), onehot,
                       precision=lax.Precision.HIGHEST) * LOG2E
    return jnp.where(dist[None] >= 0, tiles, NEG)


def _proj_res_kernel(a_ref, w_ref, r_ref, o_ref):
    a = a_ref[...]
    for c in range(0, w_ref.shape[1], 512):
        o_ref[:, c:c + 512] = r_ref[:, c:c + 512] + jnp.dot(a, w_ref[:, c:c + 512],
                                                           preferred_element_type=F32)


def _proj_residual(a, w, res, tm=512):
    m, k = a.shape
    n = w.shape[1]
    return pl.pallas_call(
        _proj_res_kernel,
        grid=(m // tm,),
        in_specs=[
            pl.BlockSpec((tm, k), lambda i: (i, 0)),
            pl.BlockSpec((k, n), lambda i: (0, 0)),
            pl.BlockSpec((tm, n), lambda i: (i, 0)),
        ],
        out_specs=pl.BlockSpec((tm, n), lambda i: (i, 0)),
        out_shape=jax.ShapeDtypeStruct((m, n), F32),
        compiler_params=_cparams(1),
        name="proj_residual",
    )(a, w, res)


def _ffn_kernel(x_ref, g_ref, wg_ref, wv_ref, cw_ref, cb_ref, wd_ref, fg_ref, o_ref,
                xn_ref, acc_ref, gbuf_ref, carry_ref, *, tiles_per_seq, final_norm):
    i = pl.program_id(0)
    f = pl.program_id(1)
    nf = pl.num_programs(1)
    tm = x_ref.shape[0]

    @pl.when(f == 0)
    def _():
        xn_ref[...] = _rms(x_ref[...], g_ref[...]).astype(BF16)
        acc_ref[...] = jnp.zeros_like(acc_ref)

    xn = xn_ref[...]
    gate = jnp.dot(xn, wg_ref[...], preferred_element_type=F32)
    val = jnp.dot(xn, wv_ref[...], preferred_element_type=F32)

    @pl.when(i % tiles_per_seq == 0)
    def _():
        carry_ref[f] = jnp.zeros(carry_ref.shape[1:], F32)

    gbuf_ref[0:8, :] = carry_ref[f]
    gbuf_ref[8:8 + tm, :] = gate
    carry_ref[f] = gate[tm - 8:tm, :]
    cw = cw_ref[0]
    y = cb_ref[0] + cw[2:3, :] * gate
    y = y + cw[1:2, :] * gbuf_ref[7:7 + tm, :]
    y = y + cw[0:1, :] * gbuf_ref[6:6 + tm, :]
    act = (y * jax.nn.sigmoid(y) * val).astype(BF16)
    acc_ref[...] += jnp.dot(act, wd_ref[...], preferred_element_type=F32)

    @pl.when(f == nf - 1)
    def _():
        h = x_ref[...] + acc_ref[...]
        if final_norm:
            h = _rms(h, fg_ref[...])
        o_ref[...] = h


def _conv_glu_ffn(x2d, g, w_gate, w_val, conv_w, conv_b, w_down, final_g, seq, final_norm, tm=1024, tf=256):
    m = x2d.shape[0]
    nf = D_FF // tf
    cw = conv_w.reshape(FFN_CONV, nf, tf).transpose(1, 0, 2)
    cb = conv_b.reshape(nf, 1, tf)
    return pl.pallas_call(
        functools.partial(_ffn_kernel, tiles_per_seq=seq // tm, final_norm=final_norm),
        grid=(m // tm, nf),
        in_specs=[
            pl.BlockSpec((tm, D_MODEL), lambda i, f: (i, 0)),
            pl.BlockSpec((1, D_MODEL), lambda i, f: (0, 0)),
            pl.BlockSpec((D_MODEL, tf), lambda i, f: (0, f)),
            pl.BlockSpec((D_MODEL, tf), lambda i, f: (0, f)),
            pl.BlockSpec((1, FFN_CONV, tf), lambda i, f: (f, 0, 0)),
            pl.BlockSpec((1, 1, tf), lambda i, f: (f, 0, 0)),
            pl.BlockSpec((tf, D_MODEL), lambda i, f: (f, 0)),
            pl.BlockSpec((1, D_MODEL), lambda i, f: (0, 0)),
        ],
        out_specs=pl.BlockSpec((tm, D_MODEL), lambda i, f: (i, 0)),
        out_shape=jax.ShapeDtypeStruct((m, D_MODEL), F32),
        scratch_shapes=[
            pltpu.VMEM((tm, D_MODEL), BF16),
            pltpu.VMEM((tm, D_MODEL), F32),
            pltpu.VMEM((tm + 8, tf), F32),
            pltpu.VMEM((nf, 8, tf), F32),
        ],
        compiler_params=_cparams(2),
        name="conv_glu_ffn",
    )(x2d, g, w_gate, w_val, cw, cb, w_down, final_g)


def _gdn_inproj_kernel(x_ref, g_ref, w_ref, wab_ref, wabt_ref, o_ref, ab_ref, abt_ref, xn_ref):
    @pl.when(pl.program_id(1) == 0)
    def _():
        xn = _rms(x_ref[...], g_ref[...])
        xn_ref[...] = xn.astype(BF16)
        ab_ref[...] = jnp.dot(xn, wab_ref[...], precision=lax.Precision.HIGHEST,
                              preferred_element_type=F32)
        abt_ref[...] = lax.dot_general(wabt_ref[...], xn, _NT, precision=lax.Precision.HIGHEST,
                                       preferred_element_type=F32)

    o_ref[...] = jnp.dot(xn_ref[...], w_ref[...], preferred_element_type=F32).astype(BF16)


def _gdn_inproj(x2d, g, w_main, w_ab, w_abt, tm=512, tn=1024):
    m = x2d.shape[0]
    n = w_main.shape[1]
    return pl.pallas_call(
        _gdn_inproj_kernel,
        grid=(m // tm, n // tn),
        in_specs=[
            pl.BlockSpec((tm, D_MODEL), lambda i, j: (i, 0)),
            pl.BlockSpec((1, D_MODEL), lambda i, j: (0, 0)),
            pl.BlockSpec((D_MODEL, tn), lambda i, j: (0, j)),
            pl.BlockSpec(w_ab.shape, lambda i, j: (0, 0)),
            pl.BlockSpec(w_abt.shape, lambda i, j: (0, 0)),
        ],
        out_specs=[
            pl.BlockSpec((tm, tn), lambda i, j: (i, j)),
            pl.BlockSpec((tm, w_ab.shape[1]), lambda i, j: (i, 0)),
            pl.BlockSpec((w_abt.shape[0], tm), lambda i, j: (0, i)),
        ],
        out_shape=[
            jax.ShapeDtypeStruct((m, n), BF16),
            jax.ShapeDtypeStruct((m, w_ab.shape[1]), F32),
            jax.ShapeDtypeStruct((w_abt.shape[0], m), F32),
        ],
        scratch_shapes=[pltpu.VMEM((tm, D_MODEL), BF16)],
        compiler_params=_cparams(2),
        name="gdn_inproj",
    )(x2d, g, w_main, w_ab, w_abt)


def _softplus(x):
    return jnp.maximum(x, 0.0) + jnp.log1p(jnp.exp(-jnp.abs(x)))


def _chunk_cumsum(x, axis):
    pos = lax.broadcasted_iota(jnp.int32, x.shape, axis) % GDN_CHUNK
    s = 1
    while s < GDN_CHUNK:
        x = x + jnp.where(pos >= s, pltpu.roll(x, s, axis), 0.0)
        s *= 2
    return x


def _gdn_prep_kernel(p_ref, cw_ref, ab_ref, abt_ref, alog_r_ref, dt_r_ref, alog_c_ref, dt_c_ref,
                     q_ref, k_ref, v_ref, gc_ref, bc_ref, gt_ref, xbuf_ref, *, tiles_per_seq):
    i = pl.program_id(0)
    tm = p_ref.shape[0]
    nh = HEADS * HEAD_DIM

    @pl.when(i % tiles_per_seq == 0)
    def _():
        xbuf_ref[0:8, :] = jnp.zeros((8, xbuf_ref.shape[1]), F32)

    xbuf_ref[8:8 + tm, :] = p_ref[...].astype(F32)
    outs = (q_ref, k_ref, v_ref)
    for s in range(3 * HEADS):
        c0 = s * HEAD_DIM
        cw = cw_ref[:, c0:c0 + HEAD_DIM]
        y = cw[3:4, :] * xbuf_ref[8:8 + tm, c0:c0 + HEAD_DIM]
        y = y + cw[2:3, :] * xbuf_ref[7:7 + tm, c0:c0 + HEAD_DIM]
        y = y + cw[1:2, :] * xbuf_ref[6:6 + tm, c0:c0 + HEAD_DIM]
        y = y + cw[0:1, :] * xbuf_ref[5:5 + tm, c0:c0 + HEAD_DIM]
        y = y * jax.nn.sigmoid(y)
        which, h = divmod(s, HEADS)
        if which < 2:
            y = y * lax.rsqrt(jnp.sum(y * y, axis=-1, keepdims=True) + L2_EPS)
            if which == 0:
                y = y * (HEAD_DIM ** -0.5)
        outs[which][:, h * HEAD_DIM:(h + 1) * HEAD_DIM] = y.astype(BF16)
    xbuf_ref[0:8, :] = xbuf_ref[tm:tm + 8, :]

    ab = ab_ref[...]
    bc_ref[...] = jax.nn.sigmoid(ab[:, :128])
    g_col = -jnp.exp(alog_r_ref[...]) * _softplus(ab[:, 128:] + dt_r_ref[...])
    gc_ref[...] = _chunk_cumsum(g_col, 0)
    abt = abt_ref[...]
    g_row = -jnp.exp(alog_c_ref[...]) * _softplus(abt[HEADS:2 * HEADS, :] + dt_c_ref[...])
    g_row = _chunk_cumsum(g_row, 1)
    for r in range(tm // GDN_CHUNK):
        gt_ref[r] = g_row[:, r * GDN_CHUNK:(r + 1) * GDN_CHUNK]


def _gdn_prep(proj, conv_w, ab, abt, alog_r, dt_r, alog_c, dt_c, seq, tm=512):
    m = proj.shape[0]
    nh = HEADS * HEAD_DIM
    tok = lambda i: (i, 0)
    const = lambda i: (0, 0)
    return pl.pallas_call(
        functools.partial(_gdn_prep_kernel, tiles_per_seq=seq // tm),
        grid=(m // tm,),
        in_specs=[
            pl.BlockSpec((tm, 3 * nh), tok),
            pl.BlockSpec(conv_w.shape, const),
            pl.BlockSpec((tm, ab.shape[1]), tok),
            pl.BlockSpec((abt.shape[0], tm), lambda i: (0, i)),
            pl.BlockSpec(alog_r.shape, const),
            pl.BlockSpec(dt_r.shape, const),
            pl.BlockSpec(alog_c.shape, const),
            pl.BlockSpec(dt_c.shape, const),
        ],
        out_specs=[
            pl.BlockSpec((tm, nh), tok),
            pl.BlockSpec((tm, nh), tok),
            pl.BlockSpec((tm, nh), tok),
            pl.BlockSpec((tm, 128), tok),
            pl.BlockSpec((tm, 128), tok),
            pl.BlockSpec((tm // GDN_CHUNK, HEADS, GDN_CHUNK), lambda i: (i, 0, 0)),
        ],
        out_shape=[
            jax.ShapeDtypeStruct((m, nh), BF16),
            jax.ShapeDtypeStruct((m, nh), BF16),
            jax.ShapeDtypeStruct((m, nh), BF16),
            jax.ShapeDtypeStruct((m, 128), F32),
            jax.ShapeDtypeStruct((m, 128), F32),
            jax.ShapeDtypeStruct((m // GDN_CHUNK, HEADS, GDN_CHUNK), F32),
        ],
        scratch_shapes=[pltpu.VMEM((tm + 8, 3 * nh), F32)],
        compiler_params=_cparams(1),
        name="gdn_prep",
    )(proj, conv_w, ab, abt, alog_r, dt_r, alog_c, dt_c)


def _inverse_masks(n):
    ri = lax.broadcasted_iota(jnp.int32, (n, n), 0)
    ci = lax.broadcasted_iota(jnp.int32, (n, n), 1)
    masks = []
    s = 1
    while s < n:
        masks.append((ri // (2 * s) == ci // (2 * s)) & ((ri // s) % 2 == 1) & ((ci // s) % 2 == 0))
        s *= 2
    return ri == ci, masks


def _unit_lower_inverses(lows, eye, masks):
    ts = [jnp.where(eye, 1.0, 0.0) - jnp.where(masks[0], low, 0.0) for low in lows]
    for off in masks[1:]:
        tbs = [t.astype(BF16) for t in ts]
        lts = [jnp.dot(jnp.where(off, low, 0.0).astype(BF16), tb, preferred_element_type=F32)
               for low, tb in zip(lows, tbs)]
        ts = [t - jnp.dot(tb, lt.astype(BF16), preferred_element_type=F32)
              for t, tb, lt in zip(ts, tbs, lts)]
    return ts


def _gdn_chunk_kernel(q_ref, k_ref, v_ref, z_ref, gc_ref, bc_ref, gt_ref, on_ref, o_ref, s_ref):
    c = GDN_CHUNK
    n_chunks = q_ref.shape[0] // c
    heads = range(HEADS)

    @pl.when(pl.program_id(1) == 0)
    def _():
        s_ref[...] = jnp.zeros_like(s_ref)

    ri = lax.broadcasted_iota(jnp.int32, (c, c), 0)
    ci = lax.broadcasted_iota(jnp.int32, (c, c), 1)
    causal = ri >= ci
    strict = ri > ci
    eye, masks = _inverse_masks(c)
    onorm = on_ref[...]

    def chunk(ic, _):
        r0 = pl.multiple_of(ic * c, c)
        rows = pl.ds(r0, c)
        hs = [slice(h * HEAD_DIM, (h + 1) * HEAD_DIM) for h in heads]
        gc_t = gc_ref[rows, :]
        bc_t = bc_ref[rows, :]
        gt_t = gt_ref[ic]
        q = [q_ref[rows, hs[h]] for h in heads]
        k = [k_ref[rows, hs[h]] for h in heads]
        gcol = [gc_t[:, h:h + 1] for h in heads]
        bcol = [bc_t[:, h:h + 1] for h in heads]
        glast = [g[c - 1:c, :] for g in gcol]
        egcol = [jnp.exp(g) for g in gcol]
        decay = [jnp.exp(jnp.where(causal, gcol[h] - gt_t[h:h + 1, :], NEG)) for h in heads]
        kq = [lax.dot_general(jnp.concatenate([k[h], q[h]], axis=0), k[h], _NT, preferred_element_type=F32)
              for h in heads]
        low = [jnp.where(strict, bcol[h] * kq[h][:c] * decay[h], 0.0) for h in heads]
        t = _unit_lower_inverses(low, eye, masks)
        kf = [x.astype(F32) for x in k]
        rhs = [jnp.concatenate([v_ref[rows, hs[h]].astype(F32) * bcol[h], kf[h] * (bcol[h] * egcol[h])],
                               axis=1).astype(BF16) for h in heads]
        sol = [jnp.dot(t[h].astype(BF16), rhs[h], preferred_element_type=F32) for h in heads]
        attn = [jnp.where(causal, kq[h][c:] * decay[h], 0.0) for h in heads]
        kdt = [(kf[h] * jnp.exp(glast[h] - gcol[h])).T for h in heads]
        state = [s_ref[h] for h in heads]
        wq = [jnp.concatenate([sol[h][:, HEAD_DIM:], q[h].astype(F32) * egcol[h]], axis=0).astype(BF16)
              for h in heads]
        ws = [jnp.dot(wq[h], state[h].astype(BF16), preferred_element_type=F32) for h in heads]
        v_new = [sol[h][:, :HEAD_DIM] - ws[h][:c] for h in heads]
        ak = [jnp.concatenate([attn[h], kdt[h]], axis=0).astype(BF16) for h in heads]
        r2 = [jnp.dot(ak[h], v_new[h].astype(BF16), preferred_element_type=F32) for h in heads]
        for h in heads:
            s_ref[h] = state[h] * jnp.exp(glast[h]) + r2[h][c:]
        for h in heads:
            z = z_ref[rows, hs[h]].astype(F32)
            o = _rms(ws[h][c:] + r2[h][:c], onorm) * (z * jax.nn.sigmoid(z))
            o_ref[rows, hs[h]] = o.astype(o_ref.dtype)
        return 0

    lax.fori_loop(0, n_chunks, chunk, 0)


def _gdn_chunks(q, k, v, proj, gc, bc, gt, o_norm, batch, seq, ts=1024):
    m = q.shape[0]
    nh = HEADS * HEAD_DIM
    tps = seq // ts
    tok = lambda b, t: (b * tps + t, 0)
    return pl.pallas_call(
        _gdn_chunk_kernel,
        grid=(batch, tps),
        in_specs=[
            pl.BlockSpec((ts, nh), tok),
            pl.BlockSpec((ts, nh), tok),
            pl.BlockSpec((ts, nh), tok),
            pl.BlockSpec((ts, nh), lambda b, t: (b * tps + t, 3)),
            pl.BlockSpec((ts, 128), tok),
            pl.BlockSpec((ts, 128), tok),
            pl.BlockSpec((ts // GDN_CHUNK, HEADS, GDN_CHUNK), lambda b, t: (b * tps + t, 0, 0)),
            pl.BlockSpec((1, HEAD_DIM), lambda b, t: (0, 0)),
        ],
        out_specs=pl.BlockSpec((ts, nh), tok),
        out_shape=jax.ShapeDtypeStruct((m, nh), BF16),
        scratch_shapes=[pltpu.VMEM((HEADS, HEAD_DIM, HEAD_DIM), F32)],
        compiler_params=_cparams(2),
        name="gdn_chunks",
    )(q, k, v, proj, gc, bc, gt, o_norm)


def _attention_layer(h2d, rel_bias, norm_g, w_qkv, w_o, batch, seq):
    nh = HEADS * HEAD_DIM
    w_qk = w_qkv[:, :2 * nh].astype(BF16)
    w_vt = w_qkv[:, 2 * nh:].T.astype(BF16)
    qk, vt = _qkv_proj(h2d, norm_g.reshape(1, -1), w_qk, w_vt, batch, seq)
    o = _moba_attention(qk.reshape(batch, seq, 2 * nh), vt, _rel_bias_tiles(rel_bias), batch, seq)
    return _proj_residual(o.reshape(batch * seq, nh), w_o.astype(BF16), h2d)


def _gdn_layer(h2d, norm_g, w_in, conv_w, a_log, dt_bias, o_norm, w_o, batch, seq):
    nh = HEADS * HEAD_DIM
    w_main = w_in[:, :4 * nh].astype(BF16)
    w_b = w_in[:, 4 * nh:4 * nh + HEADS]
    w_a = w_in[:, 4 * nh + HEADS:]
    pad = jnp.zeros((D_MODEL, 128 - HEADS), F32)
    w_ab = jnp.concatenate([w_b, pad, w_a, pad], axis=1)
    w_abt = jnp.concatenate([w_b, w_a], axis=1).T
    proj, ab, abt = _gdn_inproj(h2d, norm_g.reshape(1, -1), w_main, w_ab, w_abt)
    pad_r = jnp.zeros((128 - HEADS,), F32)
    alog_r = jnp.concatenate([a_log, pad_r]).reshape(1, 128)
    dt_r = jnp.concatenate([dt_bias, pad_r]).reshape(1, 128)
    q, k, v, gc, bc, gt = _gdn_prep(proj, conv_w, ab, abt, alog_r, dt_r,
                                    a_log.reshape(HEADS, 1), dt_bias.reshape(HEADS, 1), seq)
    o = _gdn_chunks(q, k, v, proj, gc, bc, gt, o_norm.reshape(1, -1), batch, seq)
    return _proj_residual(o, w_o.astype(BF16), h2d)


def _ffn_layer(h2d, norm_g, w_up, conv_w, conv_b, w_down, final_g, seq, final_norm):
    return _conv_glu_ffn(h2d, norm_g.reshape(1, -1), w_up[:, :D_FF].astype(BF16), w_up[:, D_FF:].astype(BF16),
                         conv_w, conv_b, w_down.astype(BF16), final_g.reshape(1, -1), seq, final_norm)


def kernel(x, rel_bias, attn_norm, attn_w_qkv, attn_w_o, gdn_norm, gdn_w_in, gdn_conv_w, gdn_a_log,
           gdn_dt_bias, gdn_o_norm, gdn_w_o, ffn_norm, ffn_w_up, ffn_conv_w, ffn_conv_b, ffn_w_down,
           final_norm):
    batch, seq, d = x.shape
    h = x.reshape(batch * seq, d)
    h = _attention_layer(h, rel_bias, attn_norm[0], attn_w_qkv[0], attn_w_o[0], batch, seq)
    h = _ffn_layer(h, ffn_norm[0], ffn_w_up[0], ffn_conv_w[0], ffn_conv_b[0], ffn_w_down[0],
                   final_norm, seq, False)
    h = _gdn_layer(h, gdn_norm[0], gdn_w_in[0], gdn_conv_w[0], gdn_a_log[0], gdn_dt_bias[0],
                   gdn_o_norm[0], gdn_w_o[0], batch, seq)
    h = _ffn_layer(h, ffn_norm[1], ffn_w_up[1], ffn_conv_w[1], ffn_conv_b[1], ffn_w_down[1],
                   final_norm, seq, True)
    return h.reshape(batch, seq, d)
```

```python
import functools
import math

import jax
import jax.numpy as jnp
from jax import lax
from jax.experimental import pallas as pl
from jax.experimental.pallas import tpu as pltpu

F32 = jnp.float32
BF16 = jnp.bfloat16

D_MODEL = 1024
HEADS = 8
HEAD_DIM = 128
MOBA_BLOCK = 256
MOBA_TOPK = 3
REL_BUCKETS = 32
REL_MAX_DIST = 1024
GDN_CONV = 4
GDN_CHUNK = 128
D_FF = 2816
FFN_CONV = 3
NORM_EPS = 1e-6
L2_EPS = 1e-6

NEG = -1e30
LOG2E = 1.4426950408889634
N_BIAS_TILES = 6
BF16_SUBLANES = 16
VT_ROWS = HEAD_DIM + BF16_SUBLANES
VMEM_LIMIT = 56 * 1024 * 1024

_NT = (((1,), (1,)), ((), ()))


def _cparams(n_axes):
    return pltpu.CompilerParams(dimension_semantics=("arbitrary",) * n_axes,
                                vmem_limit_bytes=VMEM_LIMIT)


def _rms(x, g):
    return x * lax.rsqrt(jnp.mean(x * x, axis=-1, keepdims=True) + NORM_EPS) * g


def _qkv_kernel(x_ref, g_ref, wqk_ref, wvt_ref, qk_ref, vt_ref):
    xn = _rms(x_ref[...], g_ref[...]).astype(BF16)
    tm = xn.shape[0]
    n_qk = wqk_ref.shape[1]
    for c in range(0, n_qk, 512):
        qk_ref[:, c:c + 512] = jnp.dot(xn, wqk_ref[:, c:c + 512],
                                       preferred_element_type=F32).astype(BF16)
    ones_rows = jnp.where(lax.broadcasted_iota(jnp.int32, (BF16_SUBLANES, MOBA_BLOCK), 0) == 0,
                          1.0, 0.0).astype(BF16)
    for h in range(HEADS):
        vt = lax.dot_general(wvt_ref[h * HEAD_DIM:(h + 1) * HEAD_DIM, :], xn, _NT,
                             preferred_element_type=F32)
        for r in range(tm // MOBA_BLOCK):
            vt_ref[0, h, r, 0:HEAD_DIM, :] = vt[:, r * MOBA_BLOCK:(r + 1) * MOBA_BLOCK].astype(BF16)
            vt_ref[0, h, r, HEAD_DIM:VT_ROWS, :] = ones_rows


def _qkv_proj(x2d, g, w_qk, w_vt, batch, seq, tm=512):
    m = x2d.shape[0]
    nb = seq // MOBA_BLOCK
    tps = seq // tm
    rb = tm // MOBA_BLOCK
    return pl.pallas_call(
        _qkv_kernel,
        grid=(m // tm,),
        in_specs=[
            pl.BlockSpec((tm, D_MODEL), lambda i: (i, 0)),
            pl.BlockSpec((1, D_MODEL), lambda i: (0, 0)),
            pl.BlockSpec(w_qk.shape, lambda i: (0, 0)),
            pl.BlockSpec(w_vt.shape, lambda i: (0, 0)),
        ],
        out_specs=[
            pl.BlockSpec((tm, w_qk.shape[1]), lambda i: (i, 0)),
            pl.BlockSpec((1, HEADS, rb, VT_ROWS, MOBA_BLOCK), lambda i: (i // tps, 0, i % tps, 0, 0)),
        ],
        out_shape=[
            jax.ShapeDtypeStruct((m, w_qk.shape[1]), BF16),
            jax.ShapeDtypeStruct((batch, HEADS, nb, VT_ROWS, MOBA_BLOCK), BF16),
        ],
        compiler_params=_cparams(1),
        name="qkv_proj",
    )(x2d, g, w_qk, w_vt)


def _moba_kernel(q_ref, k_ref, vt_ref, bias_ref, o_ref, kmean_ref, sel_ref, s_ref, *, nb, unroll):
    j = pl.program_id(2)
    blk = MOBA_BLOCK

    @pl.when(j == 0)
    def _():
        kf = k_ref[0].astype(F32).reshape(nb, blk, HEAD_DIM)
        kmean_ref[...] = jnp.sum(kf, axis=1) * (1.0 / blk)

    q = q_ref[0]
    gate = lax.dot_general(kmean_ref[...], q.astype(F32), _NT,
                           precision=lax.Precision.HIGHEST, preferred_element_type=F32)
    row = lax.broadcasted_iota(jnp.int32, gate.shape, 0)
    g = jnp.where(row < j, gate, NEG)
    sel = jnp.zeros(gate.shape, F32)
    for _ in range(MOBA_TOPK):
        mx = jnp.max(g, axis=0, keepdims=True)
        idx = jnp.min(jnp.where(g == mx, row, nb), axis=0, keepdims=True)
        pick = (row == idx) & (mx > 0.5 * NEG)
        sel = jnp.where(pick, 1.0, sel)
        g = jnp.where(pick, NEG, g)
    sel_ref[...] = jnp.where(row == j, 1.0, sel)

    qs = (q.astype(F32) * (HEAD_DIM ** -0.5 * LOG2E)).astype(BF16)

    def fold8(x, op):
        return op(x.reshape(blk // 8, 8, blk), axis=0)

    def block_of(t):
        tc = jnp.minimum(t, j)
        return jnp.where(tc == 0, j, tc - 1)

    def score_group(g, buf):
        gm8 = None
        for i in range(unroll):
            t = g * unroll + i
            n = block_of(t)
            kn = k_ref[0, pl.ds(pl.multiple_of(n * blk, blk), blk), :]
            s = lax.dot_general(kn, qs, _NT, preferred_element_type=F32)
            s = s + bias_ref[0, jnp.minimum(j - n, N_BIAS_TILES - 1)]
            selrow = jnp.where(t <= j, sel_ref[pl.ds(n, 1), :], 0.0)
            s = jnp.where(selrow > 0.0, s, NEG)
            s_ref[buf * unroll + i] = s
            f = fold8(s, jnp.max)
            gm8 = f if gm8 is None else jnp.maximum(gm8, f)
        return gm8

    def absorb_group(g, buf, gm8, state):
        m, acc = state
        m_new = jnp.maximum(m, jnp.max(gm8, axis=0, keepdims=True))
        acc = jnp.exp2(m - m_new) * acc
        for i in range(unroll):
            n = block_of(g * unroll + i)
            p = jnp.exp2(s_ref[buf * unroll + i] - m_new)
            acc = acc + jnp.dot(vt_ref[0, 0, n], p.astype(BF16), preferred_element_type=F32)
        return m_new, acc

    n_groups = (j + unroll) // unroll
    state = (jnp.full((1, blk), NEG, F32), jnp.zeros((VT_ROWS, blk), F32))
    gm8 = score_group(0, 0)

    def body(g, carry):
        gm8, state = carry
        state = absorb_group(g, g % 2, gm8, state)
        return score_group(g + 1, (g + 1) % 2), state

    gm8, state = lax.fori_loop(0, n_groups - 1, body, (gm8, state))
    _, acc = absorb_group(n_groups - 1, (n_groups - 1) % 2, gm8, state)
    o_ref[0] = (acc[:HEAD_DIM] / acc[HEAD_DIM:HEAD_DIM + 1]).T.astype(o_ref.dtype)


def _moba_attention(qk3, vt, bias_tiles, batch, seq, unroll=4):
    nb = seq // MOBA_BLOCK
    return pl.pallas_call(
        functools.partial(_moba_kernel, nb=nb, unroll=unroll),
        grid=(batch, HEADS, nb),
        in_specs=[
            pl.BlockSpec((1, MOBA_BLOCK, HEAD_DIM), lambda b, h, j: (b, j, h)),
            pl.BlockSpec((1, seq, HEAD_DIM), lambda b, h, j: (b, 0, HEADS + h)),
            pl.BlockSpec((1, 1, nb, VT_ROWS, MOBA_BLOCK), lambda b, h, j: (b, h, 0, 0, 0)),
            pl.BlockSpec((1, N_BIAS_TILES, MOBA_BLOCK, MOBA_BLOCK), lambda b, h, j: (h, 0, 0, 0)),
        ],
        out_specs=pl.BlockSpec((1, MOBA_BLOCK, HEAD_DIM), lambda b, h, j: (b, j, h)),
        out_shape=jax.ShapeDtypeStruct((batch, seq, HEADS * HEAD_DIM), BF16),
        scratch_shapes=[pltpu.VMEM((nb, HEAD_DIM), F32), pltpu.VMEM((nb, MOBA_BLOCK), F32),
                        pltpu.VMEM((2 * unroll, MOBA_BLOCK, MOBA_BLOCK), F32)],
        compiler_params=_cparams(3),
        name="moba_attention",
    )(qk3, qk3, vt, bias_tiles)


def _rel_bias_tiles(rel_bias):
    max_exact = REL_BUCKETS // 2
    ki = jnp.arange(MOBA_BLOCK)[:, None]
    qi = jnp.arange(MOBA_BLOCK)[None, :]
    dlt = jnp.arange(N_BIAS_TILES)[:, None, None]
    dist = dlt * MOBA_BLOCK + qi - ki
    d = jnp.maximum(dist, 0)
    large = max_exact + (jnp.log(jnp.maximum(d, 1).astype(F32) / max_exact)
                         / math.log(REL_MAX_DIST / max_exact) * (REL_BUCKETS - max_exact)).astype(jnp.int32)
    bucket = jnp.where(d < max_exact, d, jnp.minimum(large, REL_BUCKETS - 1))
    onehot = (bucket[None] == jnp.arange(REL_BUCKETS)[:, None, None, None]).astype(F32)
    tiles = jnp.einsum("hb,btkq->htkq", rel_bias.astype(F32), onehot,
                       precision=lax.Precision.HIGHEST) * LOG2E
    return jnp.where(dist[None] >= 0, tiles, NEG)


def _proj_res_kernel(a_ref, w_ref, r_ref, o_ref):
    a = a_ref[...]
    for c in range(0, w_ref.shape[1], 512):
        o_ref[:, c:c + 512] = r_ref[:, c:c + 512] + jnp.dot(a, w_ref[:, c:c + 512],
                                                           preferred_element_type=F32)


def _proj_residual(a, w, res, tm=512):
    m, k = a.shape
    n = w.shape[1]
    return pl.pallas_call(
        _proj_res_kernel,
        grid=(m // tm,),
        in_specs=[
            pl.BlockSpec((tm, k), lambda i: (i, 0)),
            pl.BlockSpec((k, n), lambda i: (0, 0)),
            pl.BlockSpec((tm, n), lambda i: (i, 0)),
        ],
        out_specs=pl.BlockSpec((tm, n), lambda i: (i, 0)),
        out_shape=jax.ShapeDtypeStruct((m, n), F32),
        compiler_params=_cparams(1),
        name="proj_residual",
    )(a, w, res)


def _ffn_kernel(x_ref, g_ref, wg_ref, wv_ref, cw_ref, cb_ref, wd_ref, fg_ref, o_ref,
                xn_ref, acc_ref, gbuf_ref, act_ref, carry_ref, *, tiles_per_seq, final_norm, sub):
    i = pl.program_id(0)
    f = pl.program_id(1)
    nf = pl.num_programs(1)
    tm = x_ref.shape[0]
    tf = wg_ref.shape[1]

    @pl.when(f == 0)
    def _():
        xn_ref[...] = _rms(x_ref[...], g_ref[...]).astype(BF16)

    @pl.when(i % tiles_per_seq == 0)
    def _():
        carry_ref[f] = jnp.zeros(carry_ref.shape[1:], F32)

    gbuf_ref[0:8, :] = carry_ref[f]
    xn = xn_ref[...]
    cw = cw_ref[0]
    cb = cb_ref[0]
    for c0 in range(0, tf, sub):
        cs = slice(c0, min(c0 + sub, tf))
        gate = jnp.dot(xn, wg_ref[:, cs], preferred_element_type=F32)
        val = jnp.dot(xn, wv_ref[:, cs], preferred_element_type=F32)
        gbuf_ref[8:8 + tm, cs] = gate
        y = cb[:, cs] + cw[2:3, cs] * gate
        y = y + cw[1:2, cs] * gbuf_ref[7:7 + tm, cs]
        y = y + cw[0:1, cs] * gbuf_ref[6:6 + tm, cs]
        act_ref[:, cs] = (y * jax.nn.sigmoid(y) * val).astype(BF16)
    carry_ref[f] = gbuf_ref[tm:tm + 8, :]
    down = jnp.dot(act_ref[...], wd_ref[...], preferred_element_type=F32)

    @pl.when(f == 0)
    def _():
        acc_ref[...] = down

    @pl.when((f > 0) & (f < nf - 1))
    def _():
        acc_ref[...] += down

    @pl.when(f == nf - 1)
    def _():
        h = x_ref[...] + acc_ref[...] + down
        if final_norm:
            h = _rms(h, fg_ref[...])
        o_ref[...] = h


def _conv_glu_ffn(x2d, g, w_gate, w_val, conv_w, conv_b, w_down, final_g, seq, final_norm,
                  tm=512, tf=1408, sub=256):
    m = x2d.shape[0]
    nf = D_FF // tf
    assert nf >= 2 and nf * tf == D_FF
    cw = conv_w.reshape(FFN_CONV, nf, tf).transpose(1, 0, 2)
    cb = conv_b.reshape(nf, 1, tf)
    return pl.pallas_call(
        functools.partial(_ffn_kernel, tiles_per_seq=seq // tm, final_norm=final_norm, sub=sub),
        grid=(m // tm, nf),
        in_specs=[
            pl.BlockSpec((tm, D_MODEL), lambda i, f: (i, 0)),
            pl.BlockSpec((1, D_MODEL), lambda i, f: (0, 0)),
            pl.BlockSpec((D_MODEL, tf), lambda i, f: (0, f)),
            pl.BlockSpec((D_MODEL, tf), lambda i, f: (0, f)),
            pl.BlockSpec((1, FFN_CONV, tf), lambda i, f: (f, 0, 0)),
            pl.BlockSpec((1, 1, tf), lambda i, f: (f, 0, 0)),
            pl.BlockSpec((tf, D_MODEL), lambda i, f: (f, 0)),
            pl.BlockSpec((1, D_MODEL), lambda i, f: (0, 0)),
        ],
        out_specs=pl.BlockSpec((tm, D_MODEL), lambda i, f: (i, 0)),
        out_shape=jax.ShapeDtypeStruct((m, D_MODEL), F32),
        scratch_shapes=[
            pltpu.VMEM((tm, D_MODEL), BF16),
            pltpu.VMEM((tm, D_MODEL), F32),
            pltpu.VMEM((tm + 8, tf), F32),
            pltpu.VMEM((tm, tf), BF16),
            pltpu.VMEM((nf, 8, tf), F32),
        ],
        compiler_params=_cparams(2),
        name="conv_glu_ffn",
    )(x2d, g, w_gate, w_val, cw, cb, w_down, final_g)


def _gdn_inproj_kernel(x_ref, g_ref, w_ref, wab_hi_ref, wab_lo_ref, o_ref, ab_ref, xn_ref):
    @pl.when(pl.program_id(1) == 0)
    def _():
        xn = _rms(x_ref[...], g_ref[...])
        xh = xn.astype(BF16)
        xn_ref[...] = xh
        xl = (xn - xh.astype(F32)).astype(BF16)
        wh = wab_hi_ref[...]
        ab_ref[...] = (jnp.dot(xh, wh, preferred_element_type=F32)
                       + jnp.dot(xh, wab_lo_ref[...], preferred_element_type=F32)
                       + jnp.dot(xl, wh, preferred_element_type=F32))

    o_ref[...] = jnp.dot(xn_ref[...], w_ref[...], preferred_element_type=F32).astype(BF16)


def _gdn_inproj(x2d, g, w_main, w_ab_hi, w_ab_lo, tm=512, tn=1024):
    m = x2d.shape[0]
    n = w_main.shape[1]
    return pl.pallas_call(
        _gdn_inproj_kernel,
        grid=(m // tm, n // tn),
        in_specs=[
            pl.BlockSpec((tm, D_MODEL), lambda i, j: (i, 0)),
            pl.BlockSpec((1, D_MODEL), lambda i, j: (0, 0)),
            pl.BlockSpec((D_MODEL, tn), lambda i, j: (0, j)),
            pl.BlockSpec(w_ab_hi.shape, lambda i, j: (0, 0)),
            pl.BlockSpec(w_ab_lo.shape, lambda i, j: (0, 0)),
        ],
        out_specs=[
            pl.BlockSpec((tm, tn), lambda i, j: (i, j)),
            pl.BlockSpec((tm, w_ab_hi.shape[1]), lambda i, j: (i, 0)),
        ],
        out_shape=[
            jax.ShapeDtypeStruct((m, n), BF16),
            jax.ShapeDtypeStruct((m, w_ab_hi.shape[1]), F32),
        ],
        scratch_shapes=[pltpu.VMEM((tm, D_MODEL), BF16)],
        compiler_params=_cparams(2),
        name="gdn_inproj",
    )(x2d, g, w_main, w_ab_hi, w_ab_lo)


def _softplus(x):
    return jnp.maximum(x, 0.0) + jnp.log1p(jnp.exp(-jnp.abs(x)))


def _chunk_cumsum(x):
    pos = lax.broadcasted_iota(jnp.int32, x.shape, 0) % GDN_CHUNK
    s = 1
    while s < GDN_CHUNK:
        x = x + jnp.where(pos >= s, pltpu.roll(x, s, 0), 0.0)
        s *= 2
    return x


def _gdn_prep_kernel(p_ref, cw_ref, ab_ref, alog_ref, dt_ref,
                     q_ref, k_ref, v_ref, gc_ref, bc_ref, gt_ref, xbuf_ref, *, tiles_per_seq):
    i = pl.program_id(0)
    tm = p_ref.shape[0]
    nh = HEADS * HEAD_DIM

    @pl.when(i % tiles_per_seq == 0)
    def _():
        xbuf_ref[0:8, :] = jnp.zeros((8, xbuf_ref.shape[1]), F32)

    xbuf_ref[8:8 + tm, :] = p_ref[...].astype(F32)
    outs = (q_ref, k_ref, v_ref)
    for s in range(3 * HEADS):
        c0 = s * HEAD_DIM
        cw = cw_ref[:, c0:c0 + HEAD_DIM]
        y = cw[3:4, :] * xbuf_ref[8:8 + tm, c0:c0 + HEAD_DIM]
        y = y + cw[2:3, :] * xbuf_ref[7:7 + tm, c0:c0 + HEAD_DIM]
        y = y + cw[1:2, :] * xbuf_ref[6:6 + tm, c0:c0 + HEAD_DIM]
        y = y + cw[0:1, :] * xbuf_ref[5:5 + tm, c0:c0 + HEAD_DIM]
        y = y * jax.nn.sigmoid(y)
        which, h = divmod(s, HEADS)
        if which < 2:
            y = y * lax.rsqrt(jnp.sum(y * y, axis=-1, keepdims=True) + L2_EPS)
            if which == 0:
                y = y * (HEAD_DIM ** -0.5)
        outs[which][:, h * HEAD_DIM:(h + 1) * HEAD_DIM] = y.astype(BF16)
    xbuf_ref[0:8, :] = xbuf_ref[tm:tm + 8, :]

    ab = ab_ref[...]
    bc_ref[...] = jax.nn.sigmoid(ab[:, :128])
    g_col = -jnp.exp(alog_ref[...]) * _softplus(ab[:, 128:] + dt_ref[...])
    gc = _chunk_cumsum(g_col)
    gc_ref[...] = gc
    g_row = gc.T
    for r in range(tm // GDN_CHUNK):
        gt_ref[r] = g_row[0:HEADS, r * GDN_CHUNK:(r + 1) * GDN_CHUNK]


def _gdn_prep(proj, conv_w, ab, alog_r, dt_r, seq, tm=512):
    m = proj.shape[0]
    nh = HEADS * HEAD_DIM
    tok = lambda i: (i, 0)
    const = lambda i: (0, 0)
    return pl.pallas_call(
        functools.partial(_gdn_prep_kernel, tiles_per_seq=seq // tm),
        grid=(m // tm,),
        in_specs=[
            pl.BlockSpec((tm, 3 * nh), tok),
            pl.BlockSpec(conv_w.shape, const),
            pl.BlockSpec((tm, ab.shape[1]), tok),
            pl.BlockSpec(alog_r.shape, const),
            pl.BlockSpec(dt_r.shape, const),
        ],
        out_specs=[
            pl.BlockSpec((tm, nh), tok),
            pl.BlockSpec((tm, nh), tok),
            pl.BlockSpec((tm, nh), tok),
            pl.BlockSpec((tm, 128), tok),
            pl.BlockSpec((tm, 128), tok),
            pl.BlockSpec((tm // GDN_CHUNK, HEADS, GDN_CHUNK), lambda i: (i, 0, 0)),
        ],
        out_shape=[
            jax.ShapeDtypeStruct((m, nh), BF16),
            jax.ShapeDtypeStruct((m, nh), BF16),
            jax.ShapeDtypeStruct((m, nh), BF16),
            jax.ShapeDtypeStruct((m, 128), F32),
            jax.ShapeDtypeStruct((m, 128), F32),
            jax.ShapeDtypeStruct((m // GDN_CHUNK, HEADS, GDN_CHUNK), F32),
        ],
        scratch_shapes=[pltpu.VMEM((tm + 8, 3 * nh), F32)],
        compiler_params=_cparams(1),
        name="gdn_prep",
    )(proj, conv_w, ab, alog_r, dt_r)


def _inverse_masks(n):
    ri = lax.broadcasted_iota(jnp.int32, (n, n), 0)
    ci = lax.broadcasted_iota(jnp.int32, (n, n), 1)
    masks = []
    s = 1
    while s < n:
        masks.append((ri // (2 * s) == ci // (2 * s)) & ((ri // s) % 2 == 1) & ((ci // s) % 2 == 0))
        s *= 2
    return ri == ci, masks


def _unit_lower_inverses(lows, eye, masks):
    ts = [jnp.where(eye, 1.0, 0.0) - jnp.where(masks[0], low, 0.0) for low in lows]
    for off in masks[1:]:
        tbs = [t.astype(BF16) for t in ts]
        lts = [jnp.dot(jnp.where(off, low, 0.0).astype(BF16), tb, preferred_element_type=F32)
               for low, tb in zip(lows, tbs)]
        ts = [t - jnp.dot(tb, lt.astype(BF16), preferred_element_type=F32)
              for t, tb, lt in zip(ts, tbs, lts)]
    return ts


def _gdn_chunk_kernel(q_ref, k_ref, v_ref, z_ref, gc_ref, bc_ref, gt_ref, on_ref, o_ref, s_ref):
    c = GDN_CHUNK
    n_chunks = q_ref.shape[0] // c
    heads = range(HEADS)

    @pl.when(pl.program_id(1) == 0)
    def _():
        s_ref[...] = jnp.zeros_like(s_ref)

    ri = lax.broadcasted_iota(jnp.int32, (c, c), 0)
    ci = lax.broadcasted_iota(jnp.int32, (c, c), 1)
    causal = ri >= ci
    strict = ri > ci
    eye, masks = _inverse_masks(c)
    onorm = on_ref[...]

    def chunk(ic, _):
        r0 = pl.multiple_of(ic * c, c)
        rows = pl.ds(r0, c)
        hs = [slice(h * HEAD_DIM, (h + 1) * HEAD_DIM) for h in heads]
        gc_t = gc_ref[rows, :]
        bc_t = bc_ref[rows, :]
        gt_t = gt_ref[ic]
        q = [q_ref[rows, hs[h]] for h in heads]
        k = [k_ref[rows, hs[h]] for h in heads]
        gcol = [gc_t[:, h:h + 1] for h in heads]
        bcol = [bc_t[:, h:h + 1] for h in heads]
        glast = [g[c - 1:c, :] for g in gcol]
        egcol = [jnp.exp(g) for g in gcol]
        decay = [jnp.exp(jnp.where(causal, gcol[h] - gt_t[h:h + 1, :], NEG)) for h in heads]
        kq = [lax.dot_general(jnp.concatenate([k[h], q[h]], axis=0), k[h], _NT, preferred_element_type=F32)
              for h in heads]
        low = [jnp.where(strict, bcol[h] * kq[h][:c] * decay[h], 0.0) for h in heads]
        t = _unit_lower_inverses(low, eye, masks)
        kf = [x.astype(F32) for x in k]
        rhs = [jnp.concatenate([v_ref[rows, hs[h]].astype(F32) * bcol[h], kf[h] * (bcol[h] * egcol[h])],
                               axis=1).astype(BF16) for h in heads]
        sol = [jnp.dot(t[h].astype(BF16), rhs[h], preferred_element_type=F32) for h in heads]
        attn = [jnp.where(causal, kq[h][c:] * decay[h], 0.0) for h in heads]
        kdt = [(kf[h] * jnp.exp(glast[h] - gcol[h])).T for h in heads]
        state = [s_ref[h] for h in heads]
        wq = [jnp.concatenate([sol[h][:, HEAD_DIM:], q[h].astype(F32) * egcol[h]], axis=0).astype(BF16)
              for h in heads]
        ws = [jnp.dot(wq[h], state[h].astype(BF16), preferred_element_type=F32) for h in heads]
        v_new = [sol[h][:, :HEAD_DIM] - ws[h][:c] for h in heads]
        ak = [jnp.concatenate([attn[h], kdt[h]], axis=0).astype(BF16) for h in heads]
        r2 = [jnp.dot(ak[h], v_new[h].astype(BF16), preferred_element_type=F32) for h in heads]
        for h in heads:
            s_ref[h] = state[h] * jnp.exp(glast[h]) + r2[h][c:]
        for h in heads:
            z = z_ref[rows, hs[h]].astype(F32)
            o = _rms(ws[h][c:] + r2[h][:c], onorm) * (z * jax.nn.sigmoid(z))
            o_ref[rows, hs[h]] = o.astype(o_ref.dtype)
        return 0

    lax.fori_loop(0, n_chunks, chunk, 0)


def _gdn_chunks(q, k, v, proj, gc, bc, gt, o_norm, batch, seq, ts=1024):
    m = q.shape[0]
    nh = HEADS * HEAD_DIM
    tps = seq // ts
    tok = lambda b, t: (b * tps + t, 0)
    return pl.pallas_call(
        _gdn_chunk_kernel,
        grid=(batch, tps),
        in_specs=[
            pl.BlockSpec((ts, nh), tok),
            pl.BlockSpec((ts, nh), tok),
            pl.BlockSpec((ts, nh), tok),
            pl.BlockSpec((ts, nh), lambda b, t: (b * tps + t, 3)),
            pl.BlockSpec((ts, 128), tok),
            pl.BlockSpec((ts, 128), tok),
            pl.BlockSpec((ts // GDN_CHUNK, HEADS, GDN_CHUNK), lambda b, t: (b * tps + t, 0, 0)),
            pl.BlockSpec((1, HEAD_DIM), lambda b, t: (0, 0)),
        ],
        out_specs=pl.BlockSpec((ts, nh), tok),
        out_shape=jax.ShapeDtypeStruct((m, nh), BF16),
        scratch_shapes=[pltpu.VMEM((HEADS, HEAD_DIM, HEAD_DIM), F32)],
        compiler_params=_cparams(2),
        name="gdn_chunks",
    )(q, k, v, proj, gc, bc, gt, o_norm)


def _attention_layer(h2d, rel_bias, norm_g, w_qkv, w_o, batch, seq):
    nh = HEADS * HEAD_DIM
    w_qk = w_qkv[:, :2 * nh].astype(BF16)
    w_vt = w_qkv[:, 2 * nh:].T.astype(BF16)
    qk, vt = _qkv_proj(h2d, norm_g.reshape(1, -1), w_qk, w_vt, batch, seq)
    o = _moba_attention(qk.reshape(batch, seq, 2 * nh), vt, _rel_bias_tiles(rel_bias), batch, seq)
    return _proj_residual(o.reshape(batch * seq, nh), w_o.astype(BF16), h2d)


def _gdn_layer(h2d, norm_g, w_in, conv_w, a_log, dt_bias, o_norm, w_o, batch, seq):
    nh = HEADS * HEAD_DIM
    w_main = w_in[:, :4 * nh].astype(BF16)
    w_b = w_in[:, 4 * nh:4 * nh + HEADS]
    w_a = w_in[:, 4 * nh + HEADS:]
    pad = jnp.zeros((D_MODEL, 128 - HEADS), F32)
    w_ab = jnp.concatenate([w_b, pad, w_a, pad], axis=1)
    w_ab_hi = w_ab.astype(BF16)
    w_ab_lo = (w_ab - w_ab_hi.astype(F32)).astype(BF16)
    proj, ab = _gdn_inproj(h2d, norm_g.reshape(1, -1), w_main, w_ab_hi, w_ab_lo)
    pad_r = jnp.zeros((128 - HEADS,), F32)
    alog_r = jnp.concatenate([a_log, pad_r]).reshape(1, 128)
    dt_r = jnp.concatenate([dt_bias, pad_r]).reshape(1, 128)
    q, k, v, gc, bc, gt = _gdn_prep(proj, conv_w, ab, alog_r, dt_r, seq)
    o = _gdn_chunks(q, k, v, proj, gc, bc, gt, o_norm.reshape(1, -1), batch, seq)
    return _proj_residual(o, w_o.astype(BF16), h2d)


def _ffn_layer(h2d, norm_g, w_up, conv_w, conv_b, w_down, final_g, seq, final_norm):
    return _conv_glu_ffn(h2d, norm_g.reshape(1, -1), w_up[:, :D_FF].astype(BF16), w_up[:, D_FF:].astype(BF16),
                         conv_w, conv_b, w_down.astype(BF16), final_g.reshape(1, -1), seq, final_norm)


def kernel(x, rel_bias, attn_norm, attn_w_qkv, attn_w_o, gdn_norm, gdn_w_in, gdn_conv_w, gdn_a_log,
           gdn_dt_bias, gdn_o_norm, gdn_w_o, ffn_norm, ffn_w_up, ffn_conv_w, ffn_conv_b, ffn_w_down,
           final_norm):
    batch, seq, d = x.shape
    h = x.reshape(batch * seq, d)
    h = _attention_layer(h, rel_bias, attn_norm[0], attn_w_qkv[0], attn_w_o[0], batch, seq)
    h = _ffn_layer(h, ffn_norm[0], ffn_w_up[0], ffn_conv_w[0], ffn_conv_b[0], ffn_w_down[0],
                   final_norm, seq, False)
    h = _gdn_layer(h, gdn_norm[0], gdn_w_in[0], gdn_conv_w[0], gdn_a_log[0], gdn_dt_bias[0],
                   gdn_o_norm[0], gdn_w_o[0], batch, seq)
    h = _ffn_layer(h, ffn_norm[1], ffn_w_up[1], ffn_conv_w[1], ffn_conv_b[1], ffn_w_down[1],
                   final_norm, seq, True)
    return h.reshape(batch, seq, d)
```

```python
import functools
import math

import jax
import jax.numpy as jnp
from jax import lax
from jax.experimental import pallas as pl
from jax.experimental.pallas import tpu as pltpu

F32 = jnp.float32
BF16 = jnp.bfloat16

D_MODEL = 1024
HEADS = 8
HEAD_DIM = 128
MOBA_BLOCK = 256
MOBA_TOPK = 3
REL_BUCKETS = 32
REL_MAX_DIST = 1024
GDN_CONV = 4
GDN_CHUNK = 128
D_FF = 2816
FFN_CONV = 3
NORM_EPS = 1e-6
L2_EPS = 1e-6

NEG = -1e30
LOG2E = 1.4426950408889634
N_BIAS_TILES = 6
BF16_SUBLANES = 16
VT_ROWS = HEAD_DIM + BF16_SUBLANES
VMEM_LIMIT = 56 * 1024 * 1024

_NT = (((1,), (1,)), ((), ()))


def _cparams(n_axes):
    return pltpu.CompilerParams(dimension_semantics=("arbitrary",) * n_axes,
                                vmem_limit_bytes=VMEM_LIMIT)


def _rms(x, g):
    return x * lax.rsqrt(jnp.mean(x * x, axis=-1, keepdims=True) + NORM_EPS) * g


def _qkv_kernel(x_ref, g_ref, wqk_ref, wvt_ref, qk_ref, vt_ref):
    xn = _rms(x_ref[...], g_ref[...]).astype(BF16)
    tm = xn.shape[0]
    n_qk = wqk_ref.shape[1]
    for c in range(0, n_qk, 512):
        qk_ref[:, c:c + 512] = jnp.dot(xn, wqk_ref[:, c:c + 512],
                                       preferred_element_type=F32).astype(BF16)
    ones_rows = jnp.where(lax.broadcasted_iota(jnp.int32, (BF16_SUBLANES, MOBA_BLOCK), 0) == 0,
                          1.0, 0.0).astype(BF16)
    for h in range(HEADS):
        vt = lax.dot_general(wvt_ref[h * HEAD_DIM:(h + 1) * HEAD_DIM, :], xn, _NT,
                             preferred_element_type=F32)
        for r in range(tm // MOBA_BLOCK):
            vt_ref[0, h, r, 0:HEAD_DIM, :] = vt[:, r * MOBA_BLOCK:(r + 1) * MOBA_BLOCK].astype(BF16)
            vt_ref[0, h, r, HEAD_DIM:VT_ROWS, :] = ones_rows


def _qkv_proj(x2d, g, w_qk, w_vt, batch, seq, tm=512):
    m = x2d.shape[0]
    nb = seq // MOBA_BLOCK
    tps = seq // tm
    rb = tm // MOBA_BLOCK
    return pl.pallas_call(
        _qkv_kernel,
        grid=(m // tm,),
        in_specs=[
            pl.BlockSpec((tm, D_MODEL), lambda i: (i, 0)),
            pl.BlockSpec((1, D_MODEL), lambda i: (0, 0)),
            pl.BlockSpec(w_qk.shape, lambda i: (0, 0)),
            pl.BlockSpec(w_vt.shape, lambda i: (0, 0)),
        ],
        out_specs=[
            pl.BlockSpec((tm, w_qk.shape[1]), lambda i: (i, 0)),
            pl.BlockSpec((1, HEADS, rb, VT_ROWS, MOBA_BLOCK), lambda i: (i // tps, 0, i % tps, 0, 0)),
        ],
        out_shape=[
            jax.ShapeDtypeStruct((m, w_qk.shape[1]), BF16),
            jax.ShapeDtypeStruct((batch, HEADS, nb, VT_ROWS, MOBA_BLOCK), BF16),
        ],
        compiler_params=_cparams(1),
        name="qkv_proj",
    )(x2d, g, w_qk, w_vt)


def _moba_kernel(q_ref, k_ref, vt_ref, bias_ref, o_ref, sel_ref, s_ref, *, nb, unroll, gate_tiles):
    blk = MOBA_BLOCK

    kmean = jnp.sum(k_ref[0].astype(F32).reshape(nb, blk, HEAD_DIM), axis=1) * (1.0 / blk)
    km_hi = kmean.astype(BF16)
    rest = kmean - km_hi.astype(F32)
    km_mid = rest.astype(BF16)
    km_lo = (rest - km_mid.astype(F32)).astype(BF16)
    km3 = jnp.concatenate([km_hi, km_mid, km_lo], axis=0)
    row = lax.broadcasted_iota(jnp.int32, (nb, blk), 0)

    def gates(tt, _):
        for i in range(gate_tiles):
            t = tt * gate_tiles + i
            qt = q_ref[0, pl.ds(pl.multiple_of(t * blk, blk), blk), :]
            g3 = lax.dot_general(km3, qt, _NT, preferred_element_type=F32)
            g = jnp.where(row < t, g3[:nb] + g3[nb:2 * nb] + g3[2 * nb:], NEG)
            sel = jnp.zeros((nb, blk), F32)
            for _ in range(MOBA_TOPK):
                mx = jnp.max(g, axis=0, keepdims=True)
                idx = jnp.min(jnp.where(g == mx, row, nb), axis=0, keepdims=True)
                pick = (row == idx) & (mx > 0.5 * NEG)
                sel = jnp.where(pick, 1.0, sel)
                g = jnp.where(pick, NEG, g)
            sel_ref[t] = jnp.where(row == t, 1.0, sel)
        return 0

    lax.fori_loop(0, nb // gate_tiles, gates, 0)

    def fold8(x, op):
        return op(x.reshape(blk // 8, 8, blk), axis=0)

    def block_of(j, t):
        tc = jnp.minimum(t, j)
        return jnp.where(tc == 0, j, tc - 1)

    def n_groups(j):
        return (j + unroll) // unroll

    def scaled_q(j):
        q = q_ref[0, pl.ds(pl.multiple_of(j * blk, blk), blk), :]
        return (q.astype(F32) * (HEAD_DIM ** -0.5 * LOG2E)).astype(BF16)

    def score_group(j, qs, g, buf):
        gm8 = None
        for i in range(unroll):
            t = g * unroll + i
            n = block_of(j, t)
            kn = k_ref[0, pl.ds(pl.multiple_of(n * blk, blk), blk), :]
            s = lax.dot_general(kn, qs, _NT, preferred_element_type=F32)
            s = s + bias_ref[0, jnp.minimum(j - n, N_BIAS_TILES - 1)]
            selrow = jnp.where(t <= j, sel_ref[j, pl.ds(n, 1), :], 0.0)
            s = jnp.where(selrow > 0.0, s, NEG)
            s_ref[buf * unroll + i] = s
            f = fold8(s, jnp.max)
            gm8 = f if gm8 is None else jnp.maximum(gm8, f)
        return gm8

    def absorb_group(j, g, buf, gm8, state):
        m, acc = state
        m_new = jnp.maximum(m, jnp.max(gm8, axis=0, keepdims=True))
        acc = jnp.exp2(m - m_new) * acc
        for i in range(unroll):
            n = block_of(j, g * unroll + i)
            p = jnp.exp2(s_ref[buf * unroll + i] - m_new)
            acc = acc + jnp.dot(vt_ref[0, 0, n], p.astype(BF16), preferred_element_type=F32)
        return m_new, acc

    def finish_tile(j, g, buf, gm8, state):
        _, acc = absorb_group(j, g, buf, gm8, state)
        out = (acc[:HEAD_DIM] / acc[HEAD_DIM:HEAD_DIM + 1]).T
        o_ref[0, pl.ds(pl.multiple_of(j * blk, blk), blk), :] = out.astype(o_ref.dtype)

    fresh = (jnp.full((1, blk), NEG, F32), jnp.zeros((VT_ROWS, blk), F32))

    def tile(j, carry):
        m, acc, gm8, buf = carry
        finish_tile(j - 1, n_groups(j - 1) - 1, buf, gm8, (m, acc))
        buf0 = 1 - buf
        qs = scaled_q(j)
        gm8 = score_group(j, qs, 0, buf0)

        def body(g, c):
            gm8, state = c
            state = absorb_group(j, g, (buf0 + g) % 2, gm8, state)
            return score_group(j, qs, g + 1, (buf0 + g + 1) % 2), state

        gm8, (m, acc) = lax.fori_loop(0, n_groups(j) - 1, body, (gm8, fresh))
        return m, acc, gm8, (buf0 + n_groups(j) - 1) % 2

    gm8 = score_group(0, scaled_q(0), 0, 0)
    m, acc, gm8, buf = lax.fori_loop(1, nb, tile, (fresh[0], fresh[1], gm8, 0))
    finish_tile(nb - 1, n_groups(nb - 1) - 1, buf, gm8, (m, acc))


def _moba_attention(qk3, vt, bias_tiles, batch, seq, unroll=4):
    nb = seq // MOBA_BLOCK
    seq_head = lambda col0: pl.BlockSpec((1, seq, HEAD_DIM), lambda b, h: (b, 0, col0 + h))
    return pl.pallas_call(
        functools.partial(_moba_kernel, nb=nb, unroll=unroll, gate_tiles=4),
        grid=(batch, HEADS),
        in_specs=[
            seq_head(0),
            seq_head(HEADS),
            pl.BlockSpec((1, 1, nb, VT_ROWS, MOBA_BLOCK), lambda b, h: (b, h, 0, 0, 0)),
            pl.BlockSpec((1, N_BIAS_TILES, MOBA_BLOCK, MOBA_BLOCK), lambda b, h: (h, 0, 0, 0)),
        ],
        out_specs=seq_head(0),
        out_shape=jax.ShapeDtypeStruct((batch, seq, HEADS * HEAD_DIM), BF16),
        scratch_shapes=[pltpu.VMEM((nb, nb, MOBA_BLOCK), F32),
                        pltpu.VMEM((2 * unroll, MOBA_BLOCK, MOBA_BLOCK), F32)],
        compiler_params=_cparams(2),
        name="moba_attention",
    )(qk3, qk3, vt, bias_tiles)


def _rel_bias_tiles(rel_bias):
    max_exact = REL_BUCKETS // 2
    ki = jnp.arange(MOBA_BLOCK)[:, None]
    qi = jnp.arange(MOBA_BLOCK)[None, :]
    dlt = jnp.arange(N_BIAS_TILES)[:, None, None]
    dist = dlt * MOBA_BLOCK + qi - ki
    d = jnp.maximum(dist, 0)
    large = max_exact + (jnp.log(jnp.maximum(d, 1).astype(F32) / max_exact)
                         / math.log(REL_MAX_DIST / max_exact) * (REL_BUCKETS - max_exact)).astype(jnp.int32)
    bucket = jnp.where(d < max_exact, d, jnp.minimum(large, REL_BUCKETS - 1))
    onehot = (bucket[None] == jnp.arange(REL_BUCKETS)[:, None, None, None]).astype(F32)
    tiles = jnp.einsum("hb,btkq->htkq", rel_bias.astype(F32), onehot,
                       precision=lax.Precision.HIGHEST) * LOG2E
    return jnp.where(dist[None] >= 0, tiles, NEG)


def _proj_res_kernel(a_ref, w_ref, r_ref, o_ref):
    a = a_ref[...]
    for c in range(0, w_ref.shape[1], 512):
        o_ref[:, c:c + 512] = r_ref[:, c:c + 512] + jnp.dot(a, w_ref[:, c:c + 512],
                                                           preferred_element_type=F32)


def _proj_residual(a, w, res, tm=512):
    m, k = a.shape
    n = w.shape[1]
    return pl.pallas_call(
        _proj_res_kernel,
        grid=(m // tm,),
        in_specs=[
            pl.BlockSpec((tm, k), lambda i: (i, 0)),
            pl.BlockSpec((k, n), lambda i: (0, 0)),
            pl.BlockSpec((tm, n), lambda i: (i, 0)),
        ],
        out_specs=pl.BlockSpec((tm, n), lambda i: (i, 0)),
        out_shape=jax.ShapeDtypeStruct((m, n), F32),
        compiler_params=_cparams(1),
        name="proj_residual",
    )(a, w, res)


def _ffn_kernel(x_ref, g_ref, wg_ref, wv_ref, cw_ref, cb_ref, wd_ref, fg_ref, o_ref,
                xn_ref, acc_ref, gbuf_ref, act_ref, carry_ref, *, tiles_per_seq, final_norm, sub):
    i = pl.program_id(0)
    f = pl.program_id(1)
    nf = pl.num_programs(1)
    tm = x_ref.shape[0]
    tf = wg_ref.shape[1]

    @pl.when(f == 0)
    def _():
        xn_ref[...] = _rms(x_ref[...], g_ref[...]).astype(BF16)

    @pl.when(i % tiles_per_seq == 0)
    def _():
        carry_ref[f] = jnp.zeros(carry_ref.shape[1:], F32)

    gbuf_ref[0:8, :] = carry_ref[f]
    xn = xn_ref[...]
    cw = cw_ref[0]
    cb = cb_ref[0]
    for c0 in range(0, tf, sub):
        cs = slice(c0, min(c0 + sub, tf))
        gate = jnp.dot(xn, wg_ref[:, cs], preferred_element_type=F32)
        val = jnp.dot(xn, wv_ref[:, cs], preferred_element_type=F32)
        gbuf_ref[8:8 + tm, cs] = gate
        y = cb[:, cs] + cw[2:3, cs] * gate
        y = y + cw[1:2, cs] * gbuf_ref[7:7 + tm, cs]
        y = y + cw[0:1, cs] * gbuf_ref[6:6 + tm, cs]
        act_ref[:, cs] = (y * jax.nn.sigmoid(y) * val).astype(BF16)
    carry_ref[f] = gbuf_ref[tm:tm + 8, :]
    down = jnp.dot(act_ref[...], wd_ref[...], preferred_element_type=F32)

    @pl.when(f == 0)
    def _():
        acc_ref[...] = down

    @pl.when((f > 0) & (f < nf - 1))
    def _():
        acc_ref[...] += down

    @pl.when(f == nf - 1)
    def _():
        h = x_ref[...] + acc_ref[...] + down
        if final_norm:
            h = _rms(h, fg_ref[...])
        o_ref[...] = h


def _conv_glu_ffn(x2d, g, w_gate, w_val, conv_w, conv_b, w_down, final_g, seq, final_norm,
                  tm=512, tf=1408, sub=256):
    m = x2d.shape[0]
    nf = D_FF // tf
    assert nf >= 2 and nf * tf == D_FF
    cw = conv_w.reshape(FFN_CONV, nf, tf).transpose(1, 0, 2)
    cb = conv_b.reshape(nf, 1, tf)
    return pl.pallas_call(
        functools.partial(_ffn_kernel, tiles_per_seq=seq // tm, final_norm=final_norm, sub=sub),
        grid=(m // tm, nf),
        in_specs=[
            pl.BlockSpec((tm, D_MODEL), lambda i, f: (i, 0)),
            pl.BlockSpec((1, D_MODEL), lambda i, f: (0, 0)),
            pl.BlockSpec((D_MODEL, tf), lambda i, f: (0, f)),
            pl.BlockSpec((D_MODEL, tf), lambda i, f: (0, f)),
            pl.BlockSpec((1, FFN_CONV, tf), lambda i, f: (f, 0, 0)),
            pl.BlockSpec((1, 1, tf), lambda i, f: (f, 0, 0)),
            pl.BlockSpec((tf, D_MODEL), lambda i, f: (f, 0)),
            pl.BlockSpec((1, D_MODEL), lambda i, f: (0, 0)),
        ],
        out_specs=pl.BlockSpec((tm, D_MODEL), lambda i, f: (i, 0)),
        out_shape=jax.ShapeDtypeStruct((m, D_MODEL), F32),
        scratch_shapes=[
            pltpu.VMEM((tm, D_MODEL), BF16),
            pltpu.VMEM((tm, D_MODEL), F32),
            pltpu.VMEM((tm + 8, tf), F32),
            pltpu.VMEM((tm, tf), BF16),
            pltpu.VMEM((nf, 8, tf), F32),
        ],
        compiler_params=_cparams(2),
        name="conv_glu_ffn",
    )(x2d, g, w_gate, w_val, cw, cb, w_down, final_g)


def _gdn_inproj_kernel(x_ref, g_ref, w_ref, wab_hi_ref, wab_lo_ref, o_ref, ab_ref, xn_ref):
    @pl.when(pl.program_id(1) == 0)
    def _():
        xn = _rms(x_ref[...], g_ref[...])
        xh = xn.astype(BF16)
        xn_ref[...] = xh
        xl = (xn - xh.astype(F32)).astype(BF16)
        wh = wab_hi_ref[...]
        ab_ref[...] = (jnp.dot(xh, wh, preferred_element_type=F32)
                       + jnp.dot(xh, wab_lo_ref[...], preferred_element_type=F32)
                       + jnp.dot(xl, wh, preferred_element_type=F32))

    o_ref[...] = jnp.dot(xn_ref[...], w_ref[...], preferred_element_type=F32).astype(BF16)


def _gdn_inproj(x2d, g, w_main, w_ab_hi, w_ab_lo, tm=512, tn=1024):
    m = x2d.shape[0]
    n = w_main.shape[1]
    return pl.pallas_call(
        _gdn_inproj_kernel,
        grid=(m // tm, n // tn),
        in_specs=[
            pl.BlockSpec((tm, D_MODEL), lambda i, j: (i, 0)),
            pl.BlockSpec((1, D_MODEL), lambda i, j: (0, 0)),
            pl.BlockSpec((D_MODEL, tn), lambda i, j: (0, j)),
            pl.BlockSpec(w_ab_hi.shape, lambda i, j: (0, 0)),
            pl.BlockSpec(w_ab_lo.shape, lambda i, j: (0, 0)),
        ],
        out_specs=[
            pl.BlockSpec((tm, tn), lambda i, j: (i, j)),
            pl.BlockSpec((tm, w_ab_hi.shape[1]), lambda i, j: (i, 0)),
        ],
        out_shape=[
            jax.ShapeDtypeStruct((m, n), BF16),
            jax.ShapeDtypeStruct((m, w_ab_hi.shape[1]), F32),
        ],
        scratch_shapes=[pltpu.VMEM((tm, D_MODEL), BF16)],
        compiler_params=_cparams(2),
        name="gdn_inproj",
    )(x2d, g, w_main, w_ab_hi, w_ab_lo)


def _softplus(x):
    return jnp.maximum(x, 0.0) + jnp.log1p(jnp.exp(-jnp.abs(x)))


def _chunk_cumsum(x):
    pos = lax.broadcasted_iota(jnp.int32, x.shape, 0) % GDN_CHUNK
    s = 1
    while s < GDN_CHUNK:
        x = x + jnp.where(pos >= s, pltpu.roll(x, s, 0), 0.0)
        s *= 2
    return x


def _gdn_prep_kernel(p_ref, cw_ref, ab_ref, alog_ref, dt_ref,
                     q_ref, k_ref, v_ref, gc_ref, bc_ref, gt_ref, xbuf_ref, *, tiles_per_seq):
    i = pl.program_id(0)
    tm = p_ref.shape[0]
    nh = HEADS * HEAD_DIM

    @pl.when(i % tiles_per_seq == 0)
    def _():
        xbuf_ref[0:8, :] = jnp.zeros((8, xbuf_ref.shape[1]), F32)

    xbuf_ref[8:8 + tm, :] = p_ref[...].astype(F32)
    outs = (q_ref, k_ref, v_ref)
    for s in range(3 * HEADS):
        c0 = s * HEAD_DIM
        cw = cw_ref[:, c0:c0 + HEAD_DIM]
        y = cw[3:4, :] * xbuf_ref[8:8 + tm, c0:c0 + HEAD_DIM]
        y = y + cw[2:3, :] * xbuf_ref[7:7 + tm, c0:c0 + HEAD_DIM]
        y = y + cw[1:2, :] * xbuf_ref[6:6 + tm, c0:c0 + HEAD_DIM]
        y = y + cw[0:1, :] * xbuf_ref[5:5 + tm, c0:c0 + HEAD_DIM]
        y = y * jax.nn.sigmoid(y)
        which, h = divmod(s, HEADS)
        if which < 2:
            y = y * lax.rsqrt(jnp.sum(y * y, axis=-1, keepdims=True) + L2_EPS)
            if which == 0:
                y = y * (HEAD_DIM ** -0.5)
        outs[which][:, h * HEAD_DIM:(h + 1) * HEAD_DIM] = y.astype(BF16)
    xbuf_ref[0:8, :] = xbuf_ref[tm:tm + 8, :]

    ab = ab_ref[...]
    bc_ref[...] = jax.nn.sigmoid(ab[:, :128])
    g_col = -jnp.exp(alog_ref[...]) * _softplus(ab[:, 128:] + dt_ref[...])
    gc = _chunk_cumsum(g_col)
    gc_ref[...] = gc
    g_row = gc.T
    for r in range(tm // GDN_CHUNK):
        gt_ref[r] = g_row[0:HEADS, r * GDN_CHUNK:(r + 1) * GDN_CHUNK]


def _gdn_prep(proj, conv_w, ab, alog_r, dt_r, seq, tm=512):
    m = proj.shape[0]
    nh = HEADS * HEAD_DIM
    tok = lambda i: (i, 0)
    const = lambda i: (0, 0)
    return pl.pallas_call(
        functools.partial(_gdn_prep_kernel, tiles_per_seq=seq // tm),
        grid=(m // tm,),
        in_specs=[
            pl.BlockSpec((tm, 3 * nh), tok),
            pl.BlockSpec(conv_w.shape, const),
            pl.BlockSpec((tm, ab.shape[1]), tok),
            pl.BlockSpec(alog_r.shape, const),
            pl.BlockSpec(dt_r.shape, const),
        ],
        out_specs=[
            pl.BlockSpec((tm, nh), tok),
            pl.BlockSpec((tm, nh), tok),
            pl.BlockSpec((tm, nh), tok),
            pl.BlockSpec((tm, 128), tok),
            pl.BlockSpec((tm, 128), tok),
            pl.BlockSpec((tm // GDN_CHUNK, HEADS, GDN_CHUNK), lambda i: (i, 0, 0)),
        ],
        out_shape=[
            jax.ShapeDtypeStruct((m, nh), BF16),
            jax.ShapeDtypeStruct((m, nh), BF16),
            jax.ShapeDtypeStruct((m, nh), BF16),
            jax.ShapeDtypeStruct((m, 128), F32),
            jax.ShapeDtypeStruct((m, 128), F32),
            jax.ShapeDtypeStruct((m // GDN_CHUNK, HEADS, GDN_CHUNK), F32),
        ],
        scratch_shapes=[pltpu.VMEM((tm + 8, 3 * nh), F32)],
        compiler_params=_cparams(1),
        name="gdn_prep",
    )(proj, conv_w, ab, alog_r, dt_r)


def _inverse_masks(n):
    ri = lax.broadcasted_iota(jnp.int32, (n, n), 0)
    ci = lax.broadcasted_iota(jnp.int32, (n, n), 1)
    masks = []
    s = 1
    while s < n:
        masks.append((ri // (2 * s) == ci // (2 * s)) & ((ri // s) % 2 == 1) & ((ci // s) % 2 == 0))
        s *= 2
    return ri == ci, masks


def _unit_lower_inverses(lows, eye, masks):
    ts = [jnp.where(eye, 1.0, 0.0) - jnp.where(masks[0], low, 0.0) for low in lows]
    for off in masks[1:]:
        tbs = [t.astype(BF16) for t in ts]
        lts = [jnp.dot(jnp.where(off, low, 0.0).astype(BF16), tb, preferred_element_type=F32)
               for low, tb in zip(lows, tbs)]
        ts = [t - jnp.dot(tb, lt.astype(BF16), preferred_element_type=F32)
              for t, tb, lt in zip(ts, tbs, lts)]
    return ts


def _gdn_chunk_kernel(q_ref, k_ref, v_ref, z_ref, gc_ref, bc_ref, gt_ref, on_ref, o_ref, s_ref):
    c = GDN_CHUNK
    n_chunks = q_ref.shape[0] // c
    heads = range(HEADS)

    @pl.when(pl.program_id(1) == 0)
    def _():
        s_ref[...] = jnp.zeros_like(s_ref)

    ri = lax.broadcasted_iota(jnp.int32, (c, c), 0)
    ci = lax.broadcasted_iota(jnp.int32, (c, c), 1)
    causal = ri >= ci
    strict = ri > ci
    eye, masks = _inverse_masks(c)
    onorm = on_ref[...]

    def chunk(ic, _):
        r0 = pl.multiple_of(ic * c, c)
        rows = pl.ds(r0, c)
        hs = [slice(h * HEAD_DIM, (h + 1) * HEAD_DIM) for h in heads]
        gc_t = gc_ref[rows, :]
        bc_t = bc_ref[rows, :]
        gt_t = gt_ref[ic]
        q = [q_ref[rows, hs[h]] for h in heads]
        k = [k_ref[rows, hs[h]] for h in heads]
        gcol = [gc_t[:, h:h + 1] for h in heads]
        bcol = [bc_t[:, h:h + 1] for h in heads]
        glast = [g[c - 1:c, :] for g in gcol]
        egcol = [jnp.exp(g) for g in gcol]
        decay = [jnp.exp(jnp.where(causal, gcol[h] - gt_t[h:h + 1, :], NEG)) for h in heads]
        kq = [lax.dot_general(jnp.concatenate([k[h], q[h]], axis=0), k[h], _NT, preferred_element_type=F32)
              for h in heads]
        low = [jnp.where(strict, bcol[h] * kq[h][:c] * decay[h], 0.0) for h in heads]
        t = _unit_lower_inverses(low, eye, masks)
        kf = [x.astype(F32) for x in k]
        rhs = [jnp.concatenate([v_ref[rows, hs[h]].astype(F32) * bcol[h], kf[h] * (bcol[h] * egcol[h])],
                               axis=1).astype(BF16) for h in heads]
        sol = [jnp.dot(t[h].astype(BF16), rhs[h], preferred_element_type=F32) for h in heads]
        attn = [jnp.where(causal, kq[h][c:] * decay[h], 0.0) for h in heads]
        kdt = [(kf[h] * jnp.exp(glast[h] - gcol[h])).T for h in heads]
        state = [s_ref[h] for h in heads]
        wq = [jnp.concatenate([sol[h][:, HEAD_DIM:], q[h].astype(F32) * egcol[h]], axis=0).astype(BF16)
              for h in heads]
        ws = [jnp.dot(wq[h], state[h].astype(BF16), preferred_element_type=F32) for h in heads]
        v_new = [sol[h][:, :HEAD_DIM] - ws[h][:c] for h in heads]
        ak = [jnp.concatenate([attn[h], kdt[h]], axis=0).astype(BF16) for h in heads]
        r2 = [jnp.dot(ak[h], v_new[h].astype(BF16), preferred_element_type=F32) for h in heads]
        for h in heads:
            s_ref[h] = state[h] * jnp.exp(glast[h]) + r2[h][c:]
        for h in heads:
            z = z_ref[rows, hs[h]].astype(F32)
            o = _rms(ws[h][c:] + r2[h][:c], onorm) * (z * jax.nn.sigmoid(z))
            o_ref[rows, hs[h]] = o.astype(o_ref.dtype)
        return 0

    lax.fori_loop(0, n_chunks, chunk, 0)


def _gdn_chunks(q, k, v, proj, gc, bc, gt, o_norm, batch, seq, ts=1024):
    m = q.shape[0]
    nh = HEADS * HEAD_DIM
    tps = seq // ts
    tok = lambda b, t: (b * tps + t, 0)
    return pl.pallas_call(
        _gdn_chunk_kernel,
        grid=(batch, tps),
        in_specs=[
            pl.BlockSpec((ts, nh), tok),
            pl.BlockSpec((ts, nh), tok),
            pl.BlockSpec((ts, nh), tok),
            pl.BlockSpec((ts, nh), lambda b, t: (b * tps + t, 3)),
            pl.BlockSpec((ts, 128), tok),
            pl.BlockSpec((ts, 128), tok),
            pl.BlockSpec((ts // GDN_CHUNK, HEADS, GDN_CHUNK), lambda b, t: (b * tps + t, 0, 0)),
            pl.BlockSpec((1, HEAD_DIM), lambda b, t: (0, 0)),
        ],
        out_specs=pl.BlockSpec((ts, nh), tok),
        out_shape=jax.ShapeDtypeStruct((m, nh), BF16),
        scratch_shapes=[pltpu.VMEM((HEADS, HEAD_DIM, HEAD_DIM), F32)],
        compiler_params=_cparams(2),
        name="gdn_chunks",
    )(q, k, v, proj, gc, bc, gt, o_norm)


def _attention_layer(h2d, rel_bias, norm_g, w_qkv, w_o, batch, seq):
    nh = HEADS * HEAD_DIM
    w_qk = w_qkv[:, :2 * nh].astype(BF16)
    w_vt = w_qkv[:, 2 * nh:].T.astype(BF16)
    qk, vt = _qkv_proj(h2d, norm_g.reshape(1, -1), w_qk, w_vt, batch, seq)
    o = _moba_attention(qk.reshape(batch, seq, 2 * nh), vt, _rel_bias_tiles(rel_bias), batch, seq)
    return _proj_residual(o.reshape(batch * seq, nh), w_o.astype(BF16), h2d)


def _gdn_layer(h2d, norm_g, w_in, conv_w, a_log, dt_bias, o_norm, w_o, batch, seq):
    nh = HEADS * HEAD_DIM
    w_main = w_in[:, :4 * nh].astype(BF16)
    w_b = w_in[:, 4 * nh:4 * nh + HEADS]
    w_a = w_in[:, 4 * nh + HEADS:]
    pad = jnp.zeros((D_MODEL, 128 - HEADS), F32)
    w_ab = jnp.concatenate([w_b, pad, w_a, pad], axis=1)
    w_ab_hi = w_ab.astype(BF16)
    w_ab_lo = (w_ab - w_ab_hi.astype(F32)).astype(BF16)
    proj, ab = _gdn_inproj(h2d, norm_g.reshape(1, -1), w_main, w_ab_hi, w_ab_lo)
    pad_r = jnp.zeros((128 - HEADS,), F32)
    alog_r = jnp.concatenate([a_log, pad_r]).reshape(1, 128)
    dt_r = jnp.concatenate([dt_bias, pad_r]).reshape(1, 128)
    q, k, v, gc, bc, gt = _gdn_prep(proj, conv_w, ab, alog_r, dt_r, seq)
    o = _gdn_chunks(q, k, v, proj, gc, bc, gt, o_norm.reshape(1, -1), batch, seq)
    return _proj_residual(o, w_o.astype(BF16), h2d)


def _ffn_layer(h2d, norm_g, w_up, conv_w, conv_b, w_down, final_g, seq, final_norm):
    return _conv_glu_ffn(h2d, norm_g.reshape(1, -1), w_up[:, :D_FF].astype(BF16), w_up[:, D_FF:].astype(BF16),
                         conv_w, conv_b, w_down.astype(BF16), final_g.reshape(1, -1), seq, final_norm)


def kernel(x, rel_bias, attn_norm, attn_w_qkv, attn_w_o, gdn_norm, gdn_w_in, gdn_conv_w, gdn_a_log,
           gdn_dt_bias, gdn_o_norm, gdn_w_o, ffn_norm, ffn_w_up, ffn_conv_w, ffn_conv_b, ffn_w_down,
           final_norm):
    batch, seq, d = x.shape
    h = x.reshape(batch * seq, d)
    h = _attention_layer(h, rel_bias, attn_norm[0], attn_w_qkv[0], attn_w_o[0], batch, seq)
    h = _ffn_layer(h, ffn_norm[0], ffn_w_up[0], ffn_conv_w[0], ffn_conv_b[0], ffn_w_down[0],
                   final_norm, seq, False)
    h = _gdn_layer(h, gdn_norm[0], gdn_w_in[0], gdn_conv_w[0], gdn_a_log[0], gdn_dt_bias[0],
                   gdn_o_norm[0], gdn_w_o[0], batch, seq)
    h = _ffn_layer(h, ffn_norm[1], ffn_w_up[1], ffn_conv_w[1], ffn_conv_b[1], ffn_w_down[1],
                   final_norm, seq, True)
    return h.reshape(batch, seq, d)
```

```python
import functools
import math

import jax
import jax.numpy as jnp
from jax import lax
from jax.experimental import pallas as pl
from jax.experimental.pallas import tpu as pltpu

F32 = jnp.float32
BF16 = jnp.bfloat16

D_MODEL = 1024
HEADS = 8
HEAD_DIM = 128
MOBA_BLOCK = 256
MOBA_TOPK = 3
REL_BUCKETS = 32
REL_MAX_DIST = 1024
GDN_CONV = 4
GDN_CHUNK = 128
D_FF = 2816
FFN_CONV = 3
NORM_EPS = 1e-6
L2_EPS = 1e-6

NEG = -1e30
LOG2E = 1.4426950408889634
N_BIAS_TILES = 6
BF16_SUBLANES = 16
VT_ROWS = HEAD_DIM + BF16_SUBLANES
VMEM_LIMIT = 56 * 1024 * 1024

_NT = (((1,), (1,)), ((), ()))


def _cparams(n_axes):
    return pltpu.CompilerParams(dimension_semantics=("arbitrary",) * n_axes,
                                vmem_limit_bytes=VMEM_LIMIT)


def _rms(x, g):
    return x * lax.rsqrt(jnp.mean(x * x, axis=-1, keepdims=True) + NORM_EPS) * g


def _silu(y):
    h = 0.5 * y
    return h + h * jnp.tanh(h)


def _qkv_kernel(x_ref, g_ref, wqk_ref, wvt_ref, qk_ref, vt_ref):
    xn = _rms(x_ref[...], g_ref[...]).astype(BF16)
    tm = xn.shape[0]
    n_qk = wqk_ref.shape[1]
    for c in range(0, n_qk, 512):
        qk_ref[:, c:c + 512] = jnp.dot(xn, wqk_ref[:, c:c + 512],
                                       preferred_element_type=F32).astype(BF16)
    ones_rows = jnp.where(lax.broadcasted_iota(jnp.int32, (BF16_SUBLANES, MOBA_BLOCK), 0) == 0,
                          1.0, 0.0).astype(BF16)
    for h in range(HEADS):
        vt = lax.dot_general(wvt_ref[h * HEAD_DIM:(h + 1) * HEAD_DIM, :], xn, _NT,
                             preferred_element_type=F32)
        for r in range(tm // MOBA_BLOCK):
            vt_ref[0, h, r, 0:HEAD_DIM, :] = vt[:, r * MOBA_BLOCK:(r + 1) * MOBA_BLOCK].astype(BF16)
            vt_ref[0, h, r, HEAD_DIM:VT_ROWS, :] = ones_rows


def _qkv_proj(x2d, g, w_qk, w_vt, batch, seq, tm=512):
    m = x2d.shape[0]
    nb = seq // MOBA_BLOCK
    tps = seq // tm
    rb = tm // MOBA_BLOCK
    return pl.pallas_call(
        _qkv_kernel,
        grid=(m // tm,),
        in_specs=[
            pl.BlockSpec((tm, D_MODEL), lambda i: (i, 0)),
            pl.BlockSpec((1, D_MODEL), lambda i: (0, 0)),
            pl.BlockSpec(w_qk.shape, lambda i: (0, 0)),
            pl.BlockSpec(w_vt.shape, lambda i: (0, 0)),
        ],
        out_specs=[
            pl.BlockSpec((tm, w_qk.shape[1]), lambda i: (i, 0)),
            pl.BlockSpec((1, HEADS, rb, VT_ROWS, MOBA_BLOCK), lambda i: (i // tps, 0, i % tps, 0, 0)),
        ],
        out_shape=[
            jax.ShapeDtypeStruct((m, w_qk.shape[1]), BF16),
            jax.ShapeDtypeStruct((batch, HEADS, nb, VT_ROWS, MOBA_BLOCK), BF16),
        ],
        compiler_params=_cparams(1),
        name="qkv_proj",
    )(x2d, g, w_qk, w_vt)


def _moba_kernel(q_ref, k_ref, vt_ref, bias_ref, o_ref, sel_ref, s_ref, *, nb, unroll, gate_tiles):
    blk = MOBA_BLOCK

    kmean = jnp.sum(k_ref[0].astype(F32).reshape(nb, blk, HEAD_DIM), axis=1) * (1.0 / blk)
    km_hi = kmean.astype(BF16)
    rest = kmean - km_hi.astype(F32)
    km_mid = rest.astype(BF16)
    km_lo = (rest - km_mid.astype(F32)).astype(BF16)
    km3 = jnp.concatenate([km_hi, km_mid, km_lo], axis=0)
    row = lax.broadcasted_iota(jnp.int32, (nb, blk), 0)

    def gates(tt, _):
        for i in range(gate_tiles):
            t = tt * gate_tiles + i
            qt = q_ref[0, pl.ds(pl.multiple_of(t * blk, blk), blk), :]
            g3 = lax.dot_general(km3, qt, _NT, preferred_element_type=F32)
            g = jnp.where(row < t, g3[:nb] + g3[nb:2 * nb] + g3[2 * nb:], NEG)
            sel = jnp.zeros((nb, blk), F32)
            for _ in range(MOBA_TOPK):
                mx = jnp.max(g, axis=0, keepdims=True)
                idx = jnp.min(jnp.where(g == mx, row, nb), axis=0, keepdims=True)
                pick = (row == idx) & (mx > 0.5 * NEG)
                sel = jnp.where(pick, 1.0, sel)
                g = jnp.where(pick, NEG, g)
            sel_ref[t] = jnp.where(row == t, 1.0, sel)
        return 0

    lax.fori_loop(0, nb // gate_tiles, gates, 0)

    def fold8(x, op):
        return op(x.reshape(blk // 8, 8, blk), axis=0)

    def block_of(j, t):
        tc = jnp.minimum(t, j)
        return jnp.where(tc == 0, j, tc - 1)

    def n_groups(j):
        return (j + unroll) // unroll

    def scaled_q(j):
        q = q_ref[0, pl.ds(pl.multiple_of(j * blk, blk), blk), :]
        return (q.astype(F32) * (HEAD_DIM ** -0.5 * LOG2E)).astype(BF16)

    def score_group(j, qs, g, buf):
        gm8 = None
        for i in range(unroll):
            t = g * unroll + i
            n = block_of(j, t)
            kn = k_ref[0, pl.ds(pl.multiple_of(n * blk, blk), blk), :]
            s = lax.dot_general(kn, qs, _NT, preferred_element_type=F32)
            s = s + bias_ref[0, jnp.minimum(j - n, N_BIAS_TILES - 1)]
            selrow = jnp.where(t <= j, sel_ref[j, pl.ds(n, 1), :], 0.0)
            s = jnp.where(selrow > 0.0, s, NEG)
            s_ref[buf * unroll + i] = s
            f = fold8(s, jnp.max)
            gm8 = f if gm8 is None else jnp.maximum(gm8, f)
        return gm8

    def absorb_group(j, g, buf, gm8, state):
        m, acc = state
        m_new = jnp.maximum(m, jnp.max(gm8, axis=0, keepdims=True))
        acc = jnp.exp2(m - m_new) * acc
        for i in range(unroll):
            n = block_of(j, g * unroll + i)
            p = jnp.exp2(s_ref[buf * unroll + i] - m_new)
            acc = acc + jnp.dot(vt_ref[0, 0, n], p.astype(BF16), preferred_element_type=F32)
        return m_new, acc

    def finish_tile(j, g, buf, gm8, state):
        _, acc = absorb_group(j, g, buf, gm8, state)
        out = (acc[:HEAD_DIM] / acc[HEAD_DIM:HEAD_DIM + 1]).T
        o_ref[0, pl.ds(pl.multiple_of(j * blk, blk), blk), :] = out.astype(o_ref.dtype)

    fresh = (jnp.full((1, blk), NEG, F32), jnp.zeros((VT_ROWS, blk), F32))

    def tile(j, carry):
        m, acc, gm8, buf = carry
        finish_tile(j - 1, n_groups(j - 1) - 1, buf, gm8, (m, acc))
        buf0 = 1 - buf
        qs = scaled_q(j)
        gm8 = score_group(j, qs, 0, buf0)

        def body(g, c):
            gm8, state = c
            state = absorb_group(j, g, (buf0 + g) % 2, gm8, state)
            return score_group(j, qs, g + 1, (buf0 + g + 1) % 2), state

        gm8, (m, acc) = lax.fori_loop(0, n_groups(j) - 1, body, (gm8, fresh))
        return m, acc, gm8, (buf0 + n_groups(j) - 1) % 2

    gm8 = score_group(0, scaled_q(0), 0, 0)
    m, acc, gm8, buf = lax.fori_loop(1, nb, tile, (fresh[0], fresh[1], gm8, 0))
    finish_tile(nb - 1, n_groups(nb - 1) - 1, buf, gm8, (m, acc))


def _moba_attention(qk3, vt, bias_tiles, batch, seq, unroll=4):
    nb = seq // MOBA_BLOCK
    seq_head = lambda col0: pl.BlockSpec((1, seq, HEAD_DIM), lambda b, h: (b, 0, col0 + h))
    return pl.pallas_call(
        functools.partial(_moba_kernel, nb=nb, unroll=unroll, gate_tiles=4),
        grid=(batch, HEADS),
        in_specs=[
            seq_head(0),
            seq_head(HEADS),
            pl.BlockSpec((1, 1, nb, VT_ROWS, MOBA_BLOCK), lambda b, h: (b, h, 0, 0, 0)),
            pl.BlockSpec((1, N_BIAS_TILES, MOBA_BLOCK, MOBA_BLOCK), lambda b, h: (h, 0, 0, 0)),
        ],
        out_specs=seq_head(0),
        out_shape=jax.ShapeDtypeStruct((batch, seq, HEADS * HEAD_DIM), BF16),
        scratch_shapes=[pltpu.VMEM((nb, nb, MOBA_BLOCK), F32),
                        pltpu.VMEM((2 * unroll, MOBA_BLOCK, MOBA_BLOCK), F32)],
        compiler_params=_cparams(2),
        name="moba_attention",
    )(qk3, qk3, vt, bias_tiles)


def _rel_bias_tiles(rel_bias):
    max_exact = REL_BUCKETS // 2
    ki = jnp.arange(MOBA_BLOCK)[:, None]
    qi = jnp.arange(MOBA_BLOCK)[None, :]
    dlt = jnp.arange(N_BIAS_TILES)[:, None, None]
    dist = dlt * MOBA_BLOCK + qi - ki
    d = jnp.maximum(dist, 0)
    large = max_exact + (jnp.log(jnp.maximum(d, 1).astype(F32) / max_exact)
                         / math.log(REL_MAX_DIST / max_exact) * (REL_BUCKETS - max_exact)).astype(jnp.int32)
    bucket = jnp.where(d < max_exact, d, jnp.minimum(large, REL_BUCKETS - 1))
    onehot = (bucket[None] == jnp.arange(REL_BUCKETS)[:, None, None, None]).astype(F32)
    tiles = jnp.einsum("hb,btkq->htkq", rel_bias.astype(F32), onehot,
                       precision=lax.Precision.HIGHEST) * LOG2E
    return jnp.where(dist[None] >= 0, tiles, NEG)


def _proj_res_kernel(a_ref, w_ref, r_ref, o_ref):
    a = a_ref[...]
    for c in range(0, w_ref.shape[1], 512):
        o_ref[:, c:c + 512] = r_ref[:, c:c + 512] + jnp.dot(a, w_ref[:, c:c + 512],
                                                           preferred_element_type=F32)


def _proj_residual(a, w, res, tm=512):
    m, k = a.shape
    n = w.shape[1]
    return pl.pallas_call(
        _proj_res_kernel,
        grid=(m // tm,),
        in_specs=[
            pl.BlockSpec((tm, k), lambda i: (i, 0)),
            pl.BlockSpec((k, n), lambda i: (0, 0)),
            pl.BlockSpec((tm, n), lambda i: (i, 0)),
        ],
        out_specs=pl.BlockSpec((tm, n), lambda i: (i, 0)),
        out_shape=jax.ShapeDtypeStruct((m, n), F32),
        compiler_params=_cparams(1),
        name="proj_residual",
    )(a, w, res)


def _ffn_kernel(x_ref, g_ref, wg_ref, wv_ref, cw_ref, cb_ref, wd_ref, fg_ref, o_ref,
                gbuf_ref, act_ref, *, tiles_per_seq, final_norm, sub):
    i = pl.program_id(0)
    tm = x_ref.shape[0]
    x = x_ref[...]
    xn = _rms(x, g_ref[...]).astype(BF16)

    @pl.when(i % tiles_per_seq == 0)
    def _():
        gbuf_ref[0:8, :] = jnp.zeros((8, gbuf_ref.shape[1]), F32)

    cw = cw_ref[...]
    cb = cb_ref[...]
    for c0 in range(0, D_FF, sub):
        cs = slice(c0, c0 + sub)
        gate = jnp.dot(xn, wg_ref[:, cs], preferred_element_type=F32)
        val = jnp.dot(xn, wv_ref[:, cs], preferred_element_type=F32)
        gbuf_ref[8:8 + tm, cs] = gate
        y = cb[:, cs] + cw[2:3, cs] * gate
        y = y + cw[1:2, cs] * gbuf_ref[7:7 + tm, cs]
        y = y + cw[0:1, cs] * gbuf_ref[6:6 + tm, cs]
        act_ref[:, cs] = (_silu(y) * val).astype(BF16)
    gbuf_ref[0:8, :] = gbuf_ref[tm:tm + 8, :]
    h = x + jnp.dot(act_ref[...], wd_ref[...], preferred_element_type=F32)
    if final_norm:
        h = _rms(h, fg_ref[...])
    o_ref[...] = h


def _conv_glu_ffn(x2d, g, w_gate, w_val, conv_w, conv_b, w_down, final_g, seq, final_norm, tm=512, sub=256):
    m = x2d.shape[0]
    assert D_FF % sub == 0
    tok = lambda i: (i, 0)
    resident = lambda shape: pl.BlockSpec(shape, lambda i: (0, 0), pipeline_mode=pl.Buffered(1))
    return pl.pallas_call(
        functools.partial(_ffn_kernel, tiles_per_seq=seq // tm, final_norm=final_norm, sub=sub),
        grid=(m // tm,),
        in_specs=[
            pl.BlockSpec((tm, D_MODEL), tok),
            resident((1, D_MODEL)),
            resident((D_MODEL, D_FF)),
            resident((D_MODEL, D_FF)),
            resident((FFN_CONV, D_FF)),
            resident((1, D_FF)),
            resident((D_FF, D_MODEL)),
            resident((1, D_MODEL)),
        ],
        out_specs=pl.BlockSpec((tm, D_MODEL), tok),
        out_shape=jax.ShapeDtypeStruct((m, D_MODEL), F32),
        scratch_shapes=[
            pltpu.VMEM((tm + 8, D_FF), F32),
            pltpu.VMEM((tm, D_FF), BF16),
        ],
        compiler_params=_cparams(1),
        name="conv_glu_ffn",
    )(x2d, g, w_gate, w_val, conv_w, conv_b.reshape(1, D_FF), w_down, final_g)


def _softplus(x):
    return jnp.maximum(x, 0.0) + jnp.log1p(jnp.exp(-jnp.abs(x)))


def _chunk_cumsum(x):
    pos = lax.broadcasted_iota(jnp.int32, x.shape, 0) % GDN_CHUNK
    s = 1
    while s < GDN_CHUNK:
        x = x + jnp.where(pos >= s, pltpu.roll(x, s, 0), 0.0)
        s *= 2
    return x


def _gdn_front_kernel(x_ref, g_ref, w_ref, wab_hi_ref, wab_lo_ref, cw_ref, alog_ref, dt_ref,
                      q_ref, k_ref, v_ref, z_ref, gc_ref, bc_ref, gt_ref, xbuf_ref, *, tiles_per_seq, sub):
    i = pl.program_id(0)
    tm = x_ref.shape[0]
    nh = HEADS * HEAD_DIM
    xn = _rms(x_ref[...], g_ref[...])
    xh = xn.astype(BF16)

    @pl.when(i % tiles_per_seq == 0)
    def _():
        xbuf_ref[0:8, :] = jnp.zeros((8, xbuf_ref.shape[1]), F32)

    for c0 in range(0, 3 * nh, sub):
        xbuf_ref[8:8 + tm, c0:c0 + sub] = jnp.dot(xh, w_ref[:, c0:c0 + sub], preferred_element_type=F32)
    for c0 in range(3 * nh, 4 * nh, sub):
        z_ref[:, c0 - 3 * nh:c0 - 3 * nh + sub] = jnp.dot(
            xh, w_ref[:, c0:c0 + sub], preferred_element_type=F32).astype(BF16)

    xl = (xn - xh.astype(F32)).astype(BF16)
    wh = wab_hi_ref[...]
    ab = (jnp.dot(xh, wh, preferred_element_type=F32)
          + jnp.dot(xh, wab_lo_ref[...], preferred_element_type=F32)
          + jnp.dot(xl, wh, preferred_element_type=F32))

    outs = (q_ref, k_ref, v_ref)
    for s in range(3 * HEADS):
        c0 = s * HEAD_DIM
        cw = cw_ref[:, c0:c0 + HEAD_DIM]
        y = cw[3:4, :] * xbuf_ref[8:8 + tm, c0:c0 + HEAD_DIM]
        y = y + cw[2:3, :] * xbuf_ref[7:7 + tm, c0:c0 + HEAD_DIM]
        y = y + cw[1:2, :] * xbuf_ref[6:6 + tm, c0:c0 + HEAD_DIM]
        y = y + cw[0:1, :] * xbuf_ref[5:5 + tm, c0:c0 + HEAD_DIM]
        y = _silu(y)
        which, h = divmod(s, HEADS)
        if which < 2:
            y = y * lax.rsqrt(jnp.sum(y * y, axis=-1, keepdims=True) + L2_EPS)
            if which == 0:
                y = y * (HEAD_DIM ** -0.5)
        outs[which][:, h * HEAD_DIM:(h + 1) * HEAD_DIM] = y.astype(BF16)
    xbuf_ref[0:8, :] = xbuf_ref[tm:tm + 8, :]

    bc_ref[...] = jax.nn.sigmoid(ab[:, :128])
    g_col = -jnp.exp(alog_ref[...]) * _softplus(ab[:, 128:] + dt_ref[...])
    gc = _chunk_cumsum(g_col)
    gc_ref[...] = gc
    g_row = gc.T
    for r in range(tm // GDN_CHUNK):
        gt_ref[r] = g_row[0:HEADS, r * GDN_CHUNK:(r + 1) * GDN_CHUNK]


def _gdn_front(x2d, g, w_main, w_ab_hi, w_ab_lo, conv_w, alog_r, dt_r, seq, tm=512, sub=512):
    m = x2d.shape[0]
    nh = HEADS * HEAD_DIM
    tok = lambda i: (i, 0)
    resident = lambda arr: pl.BlockSpec(arr.shape, lambda i: (0, 0), pipeline_mode=pl.Buffered(1))
    return pl.pallas_call(
        functools.partial(_gdn_front_kernel, tiles_per_seq=seq // tm, sub=sub),
        grid=(m // tm,),
        in_specs=[pl.BlockSpec((tm, D_MODEL), tok)]
        + [resident(a) for a in (g, w_main, w_ab_hi, w_ab_lo, conv_w, alog_r, dt_r)],
        out_specs=[
            pl.BlockSpec((tm, nh), tok),
            pl.BlockSpec((tm, nh), tok),
            pl.BlockSpec((tm, nh), tok),
            pl.BlockSpec((tm, nh), tok),
            pl.BlockSpec((tm, 128), tok),
            pl.BlockSpec((tm, 128), tok),
            pl.BlockSpec((tm // GDN_CHUNK, HEADS, GDN_CHUNK), lambda i: (i, 0, 0)),
        ],
        out_shape=[
            jax.ShapeDtypeStruct((m, nh), BF16),
            jax.ShapeDtypeStruct((m, nh), BF16),
            jax.ShapeDtypeStruct((m, nh), BF16),
            jax.ShapeDtypeStruct((m, nh), BF16),
            jax.ShapeDtypeStruct((m, 128), F32),
            jax.ShapeDtypeStruct((m, 128), F32),
            jax.ShapeDtypeStruct((m // GDN_CHUNK, HEADS, GDN_CHUNK), F32),
        ],
        scratch_shapes=[pltpu.VMEM((tm + 8, 3 * nh), F32)],
        compiler_params=_cparams(1),
        name="gdn_front",
    )(x2d, g, w_main, w_ab_hi, w_ab_lo, conv_w, alog_r, dt_r)


def _inverse_masks(n):
    ri = lax.broadcasted_iota(jnp.int32, (n, n), 0)
    ci = lax.broadcasted_iota(jnp.int32, (n, n), 1)
    masks = []
    s = 1
    while s < n:
        masks.append((ri // (2 * s) == ci // (2 * s)) & ((ri // s) % 2 == 1) & ((ci // s) % 2 == 0))
        s *= 2
    return ri == ci, masks


def _unit_lower_inverses(lows, eye, masks):
    ts = [jnp.where(eye, 1.0, 0.0) - jnp.where(masks[0], low, 0.0) for low in lows]
    for off in masks[1:]:
        tbs = [t.astype(BF16) for t in ts]
        lts = [jnp.dot(jnp.where(off, low, 0.0).astype(BF16), tb, preferred_element_type=F32)
               for low, tb in zip(lows, tbs)]
        ts = [t - jnp.dot(tb, lt.astype(BF16), preferred_element_type=F32)
              for t, tb, lt in zip(ts, tbs, lts)]
    return ts


def _gdn_chunk_kernel(q_ref, k_ref, v_ref, z_ref, gc_ref, bc_ref, gt_ref, on_ref, o_ref, s_ref):
    c = GDN_CHUNK
    n_chunks = q_ref.shape[0] // c
    heads = range(HEADS)

    @pl.when(pl.program_id(1) == 0)
    def _():
        s_ref[...] = jnp.zeros_like(s_ref)

    ri = lax.broadcasted_iota(jnp.int32, (c, c), 0)
    ci = lax.broadcasted_iota(jnp.int32, (c, c), 1)
    causal = ri >= ci
    strict = ri > ci
    eye, masks = _inverse_masks(c)
    onorm = on_ref[...]

    def chunk(ic, _):
        r0 = pl.multiple_of(ic * c, c)
        rows = pl.ds(r0, c)
        hs = [slice(h * HEAD_DIM, (h + 1) * HEAD_DIM) for h in heads]
        gc_t = gc_ref[rows, :]
        bc_t = bc_ref[rows, :]
        gt_t = gt_ref[ic]
        q = [q_ref[rows, hs[h]] for h in heads]
        k = [k_ref[rows, hs[h]] for h in heads]
        gcol = [gc_t[:, h:h + 1] for h in heads]
        bcol = [bc_t[:, h:h + 1] for h in heads]
        glast = [g[c - 1:c, :] for g in gcol]
        egcol = [jnp.exp(g) for g in gcol]
        decay = [jnp.exp(jnp.where(causal, gcol[h] - gt_t[h:h + 1, :], NEG)) for h in heads]
        kq = [lax.dot_general(jnp.concatenate([k[h], q[h]], axis=0), k[h], _NT, preferred_element_type=F32)
              for h in heads]
        low = [jnp.where(strict, bcol[h] * kq[h][:c] * decay[h], 0.0) for h in heads]
        t = _unit_lower_inverses(low, eye, masks)
        kf = [x.astype(F32) for x in k]
        rhs = [jnp.concatenate([v_ref[rows, hs[h]].astype(F32) * bcol[h], kf[h] * (bcol[h] * egcol[h])],
                               axis=1).astype(BF16) for h in heads]
        sol = [jnp.dot(t[h].astype(BF16), rhs[h], preferred_element_type=F32) for h in heads]
        attn = [jnp.where(causal, kq[h][c:] * decay[h], 0.0) for h in heads]
        kdt = [(kf[h] * jnp.exp(glast[h] - gcol[h])).T for h in heads]
        state = [s_ref[h] for h in heads]
        wq = [jnp.concatenate([sol[h][:, HEAD_DIM:], q[h].astype(F32) * egcol[h]], axis=0).astype(BF16)
              for h in heads]
        ws = [jnp.dot(wq[h], state[h].astype(BF16), preferred_element_type=F32) for h in heads]
        v_new = [sol[h][:, :HEAD_DIM] - ws[h][:c] for h in heads]
        ak = [jnp.concatenate([attn[h], kdt[h]], axis=0).astype(BF16) for h in heads]
        r2 = [jnp.dot(ak[h], v_new[h].astype(BF16), preferred_element_type=F32) for h in heads]
        for h in heads:
            s_ref[h] = state[h] * jnp.exp(glast[h]) + r2[h][c:]
        for h in heads:
            z = z_ref[rows, hs[h]].astype(F32)
            o = _rms(ws[h][c:] + r2[h][:c], onorm) * _silu(z)
            o_ref[rows, hs[h]] = o.astype(o_ref.dtype)
        return 0

    lax.fori_loop(0, n_chunks, chunk, 0)


def _gdn_chunks(q, k, v, z, gc, bc, gt, o_norm, batch, seq, ts=1024):
    m = q.shape[0]
    nh = HEADS * HEAD_DIM
    tps = seq // ts
    tok = lambda b, t: (b * tps + t, 0)
    return pl.pallas_call(
        _gdn_chunk_kernel,
        grid=(batch, tps),
        in_specs=[
            pl.BlockSpec((ts, nh), tok),
            pl.BlockSpec((ts, nh), tok),
            pl.BlockSpec((ts, nh), tok),
            pl.BlockSpec((ts, nh), tok),
            pl.BlockSpec((ts, 128), tok),
            pl.BlockSpec((ts, 128), tok),
            pl.BlockSpec((ts // GDN_CHUNK, HEADS, GDN_CHUNK), lambda b, t: (b * tps + t, 0, 0)),
            pl.BlockSpec((1, HEAD_DIM), lambda b, t: (0, 0)),
        ],
        out_specs=pl.BlockSpec((ts, nh), tok),
        out_shape=jax.ShapeDtypeStruct((m, nh), BF16),
        scratch_shapes=[pltpu.VMEM((HEADS, HEAD_DIM, HEAD_DIM), F32)],
        compiler_params=_cparams(2),
        name="gdn_chunks",
    )(q, k, v, z, gc, bc, gt, o_norm)


def _attention_layer(h2d, rel_bias, norm_g, w_qkv, w_o, batch, seq):
    nh = HEADS * HEAD_DIM
    w_qk = w_qkv[:, :2 * nh].astype(BF16)
    w_vt = w_qkv[:, 2 * nh:].T.astype(BF16)
    qk, vt = _qkv_proj(h2d, norm_g.reshape(1, -1), w_qk, w_vt, batch, seq)
    o = _moba_attention(qk.reshape(batch, seq, 2 * nh), vt, _rel_bias_tiles(rel_bias), batch, seq)
    return _proj_residual(o.reshape(batch * seq, nh), w_o.astype(BF16), h2d)


def _gdn_layer(h2d, norm_g, w_in, conv_w, a_log, dt_bias, o_norm, w_o, batch, seq):
    nh = HEADS * HEAD_DIM
    w_main = w_in[:, :4 * nh].astype(BF16)
    w_b = w_in[:, 4 * nh:4 * nh + HEADS]
    w_a = w_in[:, 4 * nh + HEADS:]
    pad = jnp.zeros((D_MODEL, 128 - HEADS), F32)
    w_ab = jnp.concatenate([w_b, pad, w_a, pad], axis=1)
    w_ab_hi = w_ab.astype(BF16)
    w_ab_lo = (w_ab - w_ab_hi.astype(F32)).astype(BF16)
    pad_r = jnp.zeros((128 - HEADS,), F32)
    alog_r = jnp.concatenate([a_log, pad_r]).reshape(1, 128)
    dt_r = jnp.concatenate([dt_bias, pad_r]).reshape(1, 128)
    q, k, v, z, gc, bc, gt = _gdn_front(h2d, norm_g.reshape(1, -1), w_main, w_ab_hi, w_ab_lo, conv_w,
                                        alog_r, dt_r, seq)
    o = _gdn_chunks(q, k, v, z, gc, bc, gt, o_norm.reshape(1, -1), batch, seq)
    return _proj_residual(o, w_o.astype(BF16), h2d)


def _ffn_layer(h2d, norm_g, w_up, conv_w, conv_b, w_down, final_g, seq, final_norm):
    return _conv_glu_ffn(h2d, norm_g.reshape(1, -1), w_up[:, :D_FF].astype(BF16), w_up[:, D_FF:].astype(BF16),
                         conv_w, conv_b, w_down.astype(BF16), final_g.reshape(1, -1), seq, final_norm)


def kernel(x, rel_bias, attn_norm, attn_w_qkv, attn_w_o, gdn_norm, gdn_w_in, gdn_conv_w, gdn_a_log,
           gdn_dt_bias, gdn_o_norm, gdn_w_o, ffn_norm, ffn_w_up, ffn_conv_w, ffn_conv_b, ffn_w_down,
           final_norm):
    batch, seq, d = x.shape
    h = x.reshape(batch * seq, d)
    h = _attention_layer(h, rel_bias, attn_norm[0], attn_w_qkv[0], attn_w_o[0], batch, seq)
    h = _ffn_layer(h, ffn_norm[0], ffn_w_up[0], ffn_conv_w[0], ffn_conv_b[0], ffn_w_down[0],
                   final_norm, seq, False)
    h = _gdn_layer(h, gdn_norm[0], gdn_w_in[0], gdn_conv_w[0], gdn_a_log[0], gdn_dt_bias[0],
                   gdn_o_norm[0], gdn_w_o[0], batch, seq)
    h = _ffn_layer(h, ffn_norm[1], ffn_w_up[1], ffn_conv_w[1], ffn_conv_b[1], ffn_w_down[1],
                   final_norm, seq, True)
    return h.reshape(batch, seq, d)
```

```python
import functools
import math

import jax
import jax.numpy as jnp
from jax import lax
from jax.experimental import pallas as pl
from jax.experimental.pallas import tpu as pltpu

F32 = jnp.float32
BF16 = jnp.bfloat16

D_MODEL = 1024
HEADS = 8
HEAD_DIM = 128
MOBA_BLOCK = 256
MOBA_TOPK = 3
REL_BUCKETS = 32
REL_MAX_DIST = 1024
GDN_CONV = 4
GDN_CHUNK = 128
D_FF = 2816
FFN_CONV = 3
NORM_EPS = 1e-6
L2_EPS = 1e-6

NEG = -1e30
LOG2E = 1.4426950408889634
N_BIAS_TILES = 6
BF16_SUBLANES = 16
VT_ROWS = HEAD_DIM + BF16_SUBLANES
VMEM_LIMIT = 56 * 1024 * 1024

_NT = (((1,), (1,)), ((), ()))


def _cparams(n_axes):
    return pltpu.CompilerParams(dimension_semantics=("arbitrary",) * n_axes,
                                vmem_limit_bytes=VMEM_LIMIT)


def _rms(x, g):
    return x * lax.rsqrt(jnp.mean(x * x, axis=-1, keepdims=True) + NORM_EPS) * g


def _silu(y):
    h = 0.5 * y
    return h + h * jnp.tanh(h)


def _qkv_kernel(x_ref, g_ref, wqk_ref, wvt_ref, qk_ref, vt_ref):
    xn = _rms(x_ref[...], g_ref[...]).astype(BF16)
    tm = xn.shape[0]
    n_qk = wqk_ref.shape[1]
    for c in range(0, n_qk, 512):
        qk_ref[:, c:c + 512] = jnp.dot(xn, wqk_ref[:, c:c + 512],
                                       preferred_element_type=F32).astype(BF16)
    ones_rows = jnp.where(lax.broadcasted_iota(jnp.int32, (BF16_SUBLANES, MOBA_BLOCK), 0) == 0,
                          1.0, 0.0).astype(BF16)
    for h in range(HEADS):
        vt = lax.dot_general(wvt_ref[h * HEAD_DIM:(h + 1) * HEAD_DIM, :], xn, _NT,
                             preferred_element_type=F32)
        for r in range(tm // MOBA_BLOCK):
            vt_ref[0, h, r, 0:HEAD_DIM, :] = vt[:, r * MOBA_BLOCK:(r + 1) * MOBA_BLOCK].astype(BF16)
            vt_ref[0, h, r, HEAD_DIM:VT_ROWS, :] = ones_rows


def _qkv_proj(x2d, g, w_qk, w_vt, batch, seq, tm=512):
    m = x2d.shape[0]
    nb = seq // MOBA_BLOCK
    tps = seq // tm
    rb = tm // MOBA_BLOCK
    return pl.pallas_call(
        _qkv_kernel,
        grid=(m // tm,),
        in_specs=[
            pl.BlockSpec((tm, D_MODEL), lambda i: (i, 0)),
            pl.BlockSpec((1, D_MODEL), lambda i: (0, 0), pipeline_mode=pl.Buffered(1)),
            pl.BlockSpec(w_qk.shape, lambda i: (0, 0), pipeline_mode=pl.Buffered(1)),
            pl.BlockSpec(w_vt.shape, lambda i: (0, 0), pipeline_mode=pl.Buffered(1)),
        ],
        out_specs=[
            pl.BlockSpec((tm, w_qk.shape[1]), lambda i: (i, 0)),
            pl.BlockSpec((1, HEADS, rb, VT_ROWS, MOBA_BLOCK), lambda i: (i // tps, 0, i % tps, 0, 0)),
        ],
        out_shape=[
            jax.ShapeDtypeStruct((m, w_qk.shape[1]), BF16),
            jax.ShapeDtypeStruct((batch, HEADS, nb, VT_ROWS, MOBA_BLOCK), BF16),
        ],
        compiler_params=_cparams(1),
        name="qkv_proj",
    )(x2d, g, w_qk, w_vt)


def _moba_kernel(q_ref, k_ref, vt_ref, bias_ref, o_ref, sel_ref, s_ref, *, nb, unroll, gate_tiles):
    blk = MOBA_BLOCK

    kmean = jnp.sum(k_ref[0].astype(F32).reshape(nb, blk, HEAD_DIM), axis=1) * (1.0 / blk)
    km_hi = kmean.astype(BF16)
    rest = kmean - km_hi.astype(F32)
    km_mid = rest.astype(BF16)
    km_lo = (rest - km_mid.astype(F32)).astype(BF16)
    km3 = jnp.concatenate([km_hi, km_mid, km_lo], axis=0)
    row = lax.broadcasted_iota(jnp.int32, (nb, blk), 0)

    def gates(tt, _):
        for i in range(gate_tiles):
            t = tt * gate_tiles + i
            qt = q_ref[0, pl.ds(pl.multiple_of(t * blk, blk), blk), :]
            g3 = lax.dot_general(km3, qt, _NT, preferred_element_type=F32)
            g = jnp.where(row < t, g3[:nb] + g3[nb:2 * nb] + g3[2 * nb:], NEG)
            sel = jnp.zeros((nb, blk), F32)
            for _ in range(MOBA_TOPK):
                mx = jnp.max(g, axis=0, keepdims=True)
                idx = jnp.min(jnp.where(g == mx, row, nb), axis=0, keepdims=True)
                pick = (row == idx) & (mx > 0.5 * NEG)
                sel = jnp.where(pick, 1.0, sel)
                g = jnp.where(pick, NEG, g)
            sel_ref[t] = jnp.where(row == t, 1.0, sel)
        return 0

    lax.fori_loop(0, nb // gate_tiles, gates, 0)

    def fold8(x, op):
        return op(x.reshape(blk // 8, 8, blk), axis=0)

    def block_of(j, t):
        tc = jnp.minimum(t, j)
        return jnp.where(tc == 0, j, tc - 1)

    def n_groups(j):
        return (j + unroll) // unroll

    def scaled_q(j):
        q = q_ref[0, pl.ds(pl.multiple_of(j * blk, blk), blk), :]
        return (q.astype(F32) * (HEAD_DIM ** -0.5 * LOG2E)).astype(BF16)

    def score_group(j, qs, g, buf):
        gm8 = None
        for i in range(unroll):
            t = g * unroll + i
            n = block_of(j, t)
            kn = k_ref[0, pl.ds(pl.multiple_of(n * blk, blk), blk), :]
            s = lax.dot_general(kn, qs, _NT, preferred_element_type=F32)
            s = s + bias_ref[0, jnp.minimum(j - n, N_BIAS_TILES - 1)]
            selrow = jnp.where(t <= j, sel_ref[j, pl.ds(n, 1), :], 0.0)
            s = jnp.where(selrow > 0.0, s, NEG)
            s_ref[buf * unroll + i] = s
            f = fold8(s, jnp.max)
            gm8 = f if gm8 is None else jnp.maximum(gm8, f)
        return gm8

    def absorb_group(j, g, buf, gm8, state):
        m, acc = state
        m_new = jnp.maximum(m, jnp.max(gm8, axis=0, keepdims=True))
        acc = jnp.exp2(m - m_new) * acc
        for i in range(unroll):
            n = block_of(j, g * unroll + i)
            p = jnp.exp2(s_ref[buf * unroll + i] - m_new)
            acc = acc + jnp.dot(vt_ref[0, 0, n], p.astype(BF16), preferred_element_type=F32)
        return m_new, acc

    def finish_tile(j, g, buf, gm8, state):
        _, acc = absorb_group(j, g, buf, gm8, state)
        out = (acc[:HEAD_DIM] / acc[HEAD_DIM:HEAD_DIM + 1]).T
        o_ref[0, pl.ds(pl.multiple_of(j * blk, blk), blk), :] = out.astype(o_ref.dtype)

    fresh = (jnp.full((1, blk), NEG, F32), jnp.zeros((VT_ROWS, blk), F32))

    def tile(j, carry):
        m, acc, gm8, buf = carry
        finish_tile(j - 1, n_groups(j - 1) - 1, buf, gm8, (m, acc))
        buf0 = 1 - buf
        qs = scaled_q(j)
        gm8 = score_group(j, qs, 0, buf0)

        def body(g, c):
            gm8, state = c
            state = absorb_group(j, g, (buf0 + g) % 2, gm8, state)
            return score_group(j, qs, g + 1, (buf0 + g + 1) % 2), state

        gm8, (m, acc) = lax.fori_loop(0, n_groups(j) - 1, body, (gm8, fresh))
        return m, acc, gm8, (buf0 + n_groups(j) - 1) % 2

    gm8 = score_group(0, scaled_q(0), 0, 0)
    m, acc, gm8, buf = lax.fori_loop(1, nb, tile, (fresh[0], fresh[1], gm8, 0))
    finish_tile(nb - 1, n_groups(nb - 1) - 1, buf, gm8, (m, acc))


def _moba_attention(qk3, vt, bias_tiles, batch, seq, unroll=4):
    nb = seq // MOBA_BLOCK
    seq_head = lambda col0: pl.BlockSpec((1, seq, HEAD_DIM), lambda b, h: (b, 0, col0 + h))
    return pl.pallas_call(
        functools.partial(_moba_kernel, nb=nb, unroll=unroll, gate_tiles=4),
        grid=(batch, HEADS),
        in_specs=[
            seq_head(0),
            seq_head(HEADS),
            pl.BlockSpec((1, 1, nb, VT_ROWS, MOBA_BLOCK), lambda b, h: (b, h, 0, 0, 0)),
            pl.BlockSpec((1, N_BIAS_TILES, MOBA_BLOCK, MOBA_BLOCK), lambda b, h: (h, 0, 0, 0)),
        ],
        out_specs=seq_head(0),
        out_shape=jax.ShapeDtypeStruct((batch, seq, HEADS * HEAD_DIM), BF16),
        scratch_shapes=[pltpu.VMEM((nb, nb, MOBA_BLOCK), F32),
                        pltpu.VMEM((2 * unroll, MOBA_BLOCK, MOBA_BLOCK), F32)],
        compiler_params=_cparams(2),
        name="moba_attention",
    )(qk3, qk3, vt, bias_tiles)


def _rel_bias_tiles(rel_bias):
    max_exact = REL_BUCKETS // 2
    ki = jnp.arange(MOBA_BLOCK)[:, None]
    qi = jnp.arange(MOBA_BLOCK)[None, :]
    dlt = jnp.arange(N_BIAS_TILES)[:, None, None]
    dist = dlt * MOBA_BLOCK + qi - ki
    d = jnp.maximum(dist, 0)
    large = max_exact + (jnp.log(jnp.maximum(d, 1).astype(F32) / max_exact)
                         / math.log(REL_MAX_DIST / max_exact) * (REL_BUCKETS - max_exact)).astype(jnp.int32)
    bucket = jnp.where(d < max_exact, d, jnp.minimum(large, REL_BUCKETS - 1))
    onehot = (bucket[None] == jnp.arange(REL_BUCKETS)[:, None, None, None]).astype(F32)
    tiles = jnp.einsum("hb,btkq->htkq", rel_bias.astype(F32), onehot,
                       precision=lax.Precision.HIGHEST) * LOG2E
    return jnp.where(dist[None] >= 0, tiles, NEG)


def _ffn_kernel(r_ref, a_ref, wo_ref, g_ref, wg_ref, wv_ref, cw_ref, cb_ref, wd_ref, fg_ref, o_ref,
                gbuf_ref, act_ref, *, tiles_per_seq, final_norm, sub):
    i = pl.program_id(0)
    tm = r_ref.shape[0]
    x = r_ref[...] + jnp.dot(a_ref[...], wo_ref[...], preferred_element_type=F32)
    xn = _rms(x, g_ref[...]).astype(BF16)

    @pl.when(i % tiles_per_seq == 0)
    def _():
        gbuf_ref[0:8, :] = jnp.zeros((8, gbuf_ref.shape[1]), F32)

    cw = cw_ref[...]
    cb = cb_ref[...]
    for c0 in range(0, D_FF, sub):
        cs = slice(c0, c0 + sub)
        gate = jnp.dot(xn, wg_ref[:, cs], preferred_element_type=F32)
        val = jnp.dot(xn, wv_ref[:, cs], preferred_element_type=F32)
        gbuf_ref[8:8 + tm, cs] = gate
        y = cb[:, cs] + cw[2:3, cs] * gate
        y = y + cw[1:2, cs] * gbuf_ref[7:7 + tm, cs]
        y = y + cw[0:1, cs] * gbuf_ref[6:6 + tm, cs]
        act_ref[:, cs] = (_silu(y) * val).astype(BF16)
    gbuf_ref[0:8, :] = gbuf_ref[tm:tm + 8, :]
    h = x + jnp.dot(act_ref[...], wd_ref[...], preferred_element_type=F32)
    if final_norm:
        h = _rms(h, fg_ref[...])
    o_ref[...] = h


def _mixer_out_ffn(res, a, w_o, g, w_gate, w_val, conv_w, conv_b, w_down, final_g, seq, final_norm,
                   tm=512, sub=256):
    m = res.shape[0]
    assert D_FF % sub == 0
    tok = lambda i: (i, 0)
    resident = lambda shape: pl.BlockSpec(shape, lambda i: (0, 0), pipeline_mode=pl.Buffered(1))
    return pl.pallas_call(
        functools.partial(_ffn_kernel, tiles_per_seq=seq // tm, final_norm=final_norm, sub=sub),
        grid=(m // tm,),
        in_specs=[
            pl.BlockSpec((tm, D_MODEL), tok),
            pl.BlockSpec((tm, a.shape[1]), tok),
            resident(w_o.shape),
            resident((1, D_MODEL)),
            resident((D_MODEL, D_FF)),
            resident((D_MODEL, D_FF)),
            resident((FFN_CONV, D_FF)),
            resident((1, D_FF)),
            resident((D_FF, D_MODEL)),
            resident((1, D_MODEL)),
        ],
        out_specs=pl.BlockSpec((tm, D_MODEL), tok),
        out_shape=jax.ShapeDtypeStruct((m, D_MODEL), F32),
        scratch_shapes=[
            pltpu.VMEM((tm + 8, D_FF), F32),
            pltpu.VMEM((tm, D_FF), BF16),
        ],
        compiler_params=_cparams(1),
        name="mixer_out_ffn",
    )(res, a, w_o, g, w_gate, w_val, conv_w, conv_b.reshape(1, D_FF), w_down, final_g)


def _softplus(x):
    return jnp.maximum(x, 0.0) + jnp.log1p(jnp.exp(-jnp.abs(x)))


def _chunk_cumsum(x):
    pos = lax.broadcasted_iota(jnp.int32, x.shape, 0) % GDN_CHUNK
    s = 1
    while s < GDN_CHUNK:
        x = x + jnp.where(pos >= s, pltpu.roll(x, s, 0), 0.0)
        s *= 2
    return x


def _gdn_front_kernel(x_ref, g_ref, w_ref, wab_hi_ref, wab_lo_ref, cw_ref, alog_ref, dt_ref,
                      q_ref, k_ref, v_ref, z_ref, gc_ref, bc_ref, gt_ref, xbuf_ref, *, tiles_per_seq, sub):
    i = pl.program_id(0)
    tm = x_ref.shape[0]
    nh = HEADS * HEAD_DIM
    xn = _rms(x_ref[...], g_ref[...])
    xh = xn.astype(BF16)

    @pl.when(i % tiles_per_seq == 0)
    def _():
        xbuf_ref[0:8, :] = jnp.zeros((8, xbuf_ref.shape[1]), F32)

    for c0 in range(0, 3 * nh, sub):
        xbuf_ref[8:8 + tm, c0:c0 + sub] = jnp.dot(xh, w_ref[:, c0:c0 + sub], preferred_element_type=F32)
    for c0 in range(3 * nh, 4 * nh, sub):
        z_ref[:, c0 - 3 * nh:c0 - 3 * nh + sub] = jnp.dot(
            xh, w_ref[:, c0:c0 + sub], preferred_element_type=F32).astype(BF16)

    xl = (xn - xh.astype(F32)).astype(BF16)
    wh = wab_hi_ref[...]
    ab = (jnp.dot(xh, wh, preferred_element_type=F32)
          + jnp.dot(xh, wab_lo_ref[...], preferred_element_type=F32)
          + jnp.dot(xl, wh, preferred_element_type=F32))

    outs = (q_ref, k_ref, v_ref)
    for s in range(3 * HEADS):
        c0 = s * HEAD_DIM
        cw = cw_ref[:, c0:c0 + HEAD_DIM]
        y = cw[3:4, :] * xbuf_ref[8:8 + tm, c0:c0 + HEAD_DIM]
        y = y + cw[2:3, :] * xbuf_ref[7:7 + tm, c0:c0 + HEAD_DIM]
        y = y + cw[1:2, :] * xbuf_ref[6:6 + tm, c0:c0 + HEAD_DIM]
        y = y + cw[0:1, :] * xbuf_ref[5:5 + tm, c0:c0 + HEAD_DIM]
        y = _silu(y)
        which, h = divmod(s, HEADS)
        if which < 2:
            y = y * lax.rsqrt(jnp.sum(y * y, axis=-1, keepdims=True) + L2_EPS)
            if which == 0:
                y = y * (HEAD_DIM ** -0.5)
        outs[which][:, h * HEAD_DIM:(h + 1) * HEAD_DIM] = y.astype(BF16)
    xbuf_ref[0:8, :] = xbuf_ref[tm:tm + 8, :]

    bc_ref[...] = jax.nn.sigmoid(ab[:, :128])
    g_col = -jnp.exp(alog_ref[...]) * _softplus(ab[:, 128:] + dt_ref[...])
    gc = _chunk_cumsum(g_col)
    gc_ref[...] = gc
    g_row = gc.T
    for r in range(tm // GDN_CHUNK):
        gt_ref[r] = g_row[0:HEADS, r * GDN_CHUNK:(r + 1) * GDN_CHUNK]


def _gdn_front(x2d, g, w_main, w_ab_hi, w_ab_lo, conv_w, alog_r, dt_r, seq, tm=512, sub=512):
    m = x2d.shape[0]
    nh = HEADS * HEAD_DIM
    tok = lambda i: (i, 0)
    resident = lambda arr: pl.BlockSpec(arr.shape, lambda i: (0, 0), pipeline_mode=pl.Buffered(1))
    return pl.pallas_call(
        functools.partial(_gdn_front_kernel, tiles_per_seq=seq // tm, sub=sub),
        grid=(m // tm,),
        in_specs=[pl.BlockSpec((tm, D_MODEL), tok)]
        + [resident(a) for a in (g, w_main, w_ab_hi, w_ab_lo, conv_w, alog_r, dt_r)],
        out_specs=[
            pl.BlockSpec((tm, nh), tok),
            pl.BlockSpec((tm, nh), tok),
            pl.BlockSpec((tm, nh), tok),
            pl.BlockSpec((tm, nh), tok),
            pl.BlockSpec((tm, 128), tok),
            pl.BlockSpec((tm, 128), tok),
            pl.BlockSpec((tm // GDN_CHUNK, HEADS, GDN_CHUNK), lambda i: (i, 0, 0)),
        ],
        out_shape=[
            jax.ShapeDtypeStruct((m, nh), BF16),
            jax.ShapeDtypeStruct((m, nh), BF16),
            jax.ShapeDtypeStruct((m, nh), BF16),
            jax.ShapeDtypeStruct((m, nh), BF16),
            jax.ShapeDtypeStruct((m, 128), F32),
            jax.ShapeDtypeStruct((m, 128), F32),
            jax.ShapeDtypeStruct((m // GDN_CHUNK, HEADS, GDN_CHUNK), F32),
        ],
        scratch_shapes=[pltpu.VMEM((tm + 8, 3 * nh), F32)],
        compiler_params=_cparams(1),
        name="gdn_front",
    )(x2d, g, w_main, w_ab_hi, w_ab_lo, conv_w, alog_r, dt_r)


def _inverse_masks(n):
    ri = lax.broadcasted_iota(jnp.int32, (n, n), 0)
    ci = lax.broadcasted_iota(jnp.int32, (n, n), 1)
    masks = []
    s = 1
    while s < n:
        masks.append((ri // (2 * s) == ci // (2 * s)) & ((ri // s) % 2 == 1) & ((ci // s) % 2 == 0))
        s *= 2
    return ri == ci, masks


def _unit_lower_inverses(lows, eye, masks):
    ts = [jnp.where(eye, 1.0, 0.0) - jnp.where(masks[0], low, 0.0) for low in lows]
    for off in masks[1:]:
        tbs = [t.astype(BF16) for t in ts]
        lts = [jnp.dot(jnp.where(off, low, 0.0).astype(BF16), tb, preferred_element_type=F32)
               for low, tb in zip(lows, tbs)]
        ts = [t - jnp.dot(tb, lt.astype(BF16), preferred_element_type=F32)
              for t, tb, lt in zip(ts, tbs, lts)]
    return ts


def _gdn_chunk_kernel(q_ref, k_ref, v_ref, z_ref, gc_ref, bc_ref, gt_ref, on_ref, o_ref, s_ref, *, group):
    c = GDN_CHUNK
    n_chunks = q_ref.shape[0] // c
    hs = [slice(h * HEAD_DIM, (h + 1) * HEAD_DIM) for h in range(HEADS)]

    @pl.when(pl.program_id(1) == 0)
    def _():
        s_ref[...] = jnp.zeros_like(s_ref)

    ri = lax.broadcasted_iota(jnp.int32, (c, c), 0)
    ci = lax.broadcasted_iota(jnp.int32, (c, c), 1)
    causal = ri >= ci
    strict = ri > ci
    eye, masks = _inverse_masks(c)
    onorm = on_ref[...]

    def chunks(ig, _):
        rows, gcol, bcol, gt_rows, items = [], [], [], [], []
        for j in range(group):
            ic = ig * group + j
            rows.append(pl.ds(pl.multiple_of(ic * c, c), c))
            gc_t = gc_ref[rows[j], :]
            bc_t = bc_ref[rows[j], :]
            gt_t = gt_ref[ic]
            for h in range(HEADS):
                items.append((j, h))
                gcol.append(gc_t[:, h:h + 1])
                bcol.append(bc_t[:, h:h + 1])
                gt_rows.append(gt_t[h:h + 1, :])
        n = range(len(items))
        q = [q_ref[rows[j], hs[h]] for j, h in items]
        k = [k_ref[rows[j], hs[h]] for j, h in items]
        glast = [g[c - 1:c, :] for g in gcol]
        egcol = [jnp.exp(g) for g in gcol]
        decay = [jnp.exp(jnp.where(causal, gcol[i] - gt_rows[i], NEG)) for i in n]
        kq = [lax.dot_general(jnp.concatenate([k[i], q[i]], axis=0), k[i], _NT, preferred_element_type=F32)
              for i in n]
        low = [jnp.where(strict, bcol[i] * kq[i][:c] * decay[i], 0.0) for i in n]
        t = _unit_lower_inverses(low, eye, masks)
        kf = [x.astype(F32) for x in k]
        rhs = [jnp.concatenate([v_ref[rows[j], hs[h]].astype(F32) * bcol[i], kf[i] * (bcol[i] * egcol[i])],
                               axis=1).astype(BF16) for i, (j, h) in enumerate(items)]
        sol = [jnp.dot(t[i].astype(BF16), rhs[i], preferred_element_type=F32) for i in n]
        attn = [jnp.where(causal, kq[i][c:] * decay[i], 0.0) for i in n]
        kdt = [(kf[i] * jnp.exp(glast[i] - gcol[i])).T for i in n]
        wq = [jnp.concatenate([sol[i][:, HEAD_DIM:], q[i].astype(F32) * egcol[i]], axis=0).astype(BF16)
              for i in n]
        ak = [jnp.concatenate([attn[i], kdt[i]], axis=0).astype(BF16) for i in n]
        state = [s_ref[h] for h in range(HEADS)]
        for j in range(group):
            idx = [j * HEADS + h for h in range(HEADS)]
            ws = [jnp.dot(wq[i], state[h].astype(BF16), preferred_element_type=F32)
                  for h, i in enumerate(idx)]
            v_new = [sol[i][:, :HEAD_DIM] - ws[h][:c] for h, i in enumerate(idx)]
            r2 = [jnp.dot(ak[i], v_new[h].astype(BF16), preferred_element_type=F32)
                  for h, i in enumerate(idx)]
            state = [state[h] * jnp.exp(glast[i]) + r2[h][c:] for h, i in enumerate(idx)]
            for h in range(HEADS):
                z = z_ref[rows[j], hs[h]].astype(F32)
                o = _rms(ws[h][c:] + r2[h][:c], onorm) * _silu(z)
                o_ref[rows[j], hs[h]] = o.astype(o_ref.dtype)
        for h in range(HEADS):
            s_ref[h] = state[h]
        return 0

    lax.fori_loop(0, n_chunks // group, chunks, 0)


def _gdn_chunks(q, k, v, z, gc, bc, gt, o_norm, batch, seq, ts=1024, group=2):
    m = q.shape[0]
    nh = HEADS * HEAD_DIM
    tps = seq // ts
    assert (ts // GDN_CHUNK) % group == 0
    tok = lambda b, t: (b * tps + t, 0)
    return pl.pallas_call(
        functools.partial(_gdn_chunk_kernel, group=group),
        grid=(batch, tps),
        in_specs=[
            pl.BlockSpec((ts, nh), tok),
            pl.BlockSpec((ts, nh), tok),
            pl.BlockSpec((ts, nh), tok),
            pl.BlockSpec((ts, nh), tok),
            pl.BlockSpec((ts, 128), tok),
            pl.BlockSpec((ts, 128), tok),
            pl.BlockSpec((ts // GDN_CHUNK, HEADS, GDN_CHUNK), lambda b, t: (b * tps + t, 0, 0)),
            pl.BlockSpec((1, HEAD_DIM), lambda b, t: (0, 0)),
        ],
        out_specs=pl.BlockSpec((ts, nh), tok),
        out_shape=jax.ShapeDtypeStruct((m, nh), BF16),
        scratch_shapes=[pltpu.VMEM((HEADS, HEAD_DIM, HEAD_DIM), F32)],
        compiler_params=_cparams(2),
        name="gdn_chunks",
    )(q, k, v, z, gc, bc, gt, o_norm)


def _attention_mixer(h2d, rel_bias, norm_g, w_qkv, batch, seq):
    nh = HEADS * HEAD_DIM
    w_qk = w_qkv[:, :2 * nh].astype(BF16)
    w_vt = w_qkv[:, 2 * nh:].T.astype(BF16)
    qk, vt = _qkv_proj(h2d, norm_g.reshape(1, -1), w_qk, w_vt, batch, seq)
    o = _moba_attention(qk.reshape(batch, seq, 2 * nh), vt, _rel_bias_tiles(rel_bias), batch, seq)
    return o.reshape(batch * seq, nh)


def _gdn_mixer(h2d, norm_g, w_in, conv_w, a_log, dt_bias, o_norm, batch, seq):
    nh = HEADS * HEAD_DIM
    w_main = w_in[:, :4 * nh].astype(BF16)
    w_b = w_in[:, 4 * nh:4 * nh + HEADS]
    w_a = w_in[:, 4 * nh + HEADS:]
    pad = jnp.zeros((D_MODEL, 128 - HEADS), F32)
    w_ab = jnp.concatenate([w_b, pad, w_a, pad], axis=1)
    w_ab_hi = w_ab.astype(BF16)
    w_ab_lo = (w_ab - w_ab_hi.astype(F32)).astype(BF16)
    pad_r = jnp.zeros((128 - HEADS,), F32)
    alog_r = jnp.concatenate([a_log, pad_r]).reshape(1, 128)
    dt_r = jnp.concatenate([dt_bias, pad_r]).reshape(1, 128)
    q, k, v, z, gc, bc, gt = _gdn_front(h2d, norm_g.reshape(1, -1), w_main, w_ab_hi, w_ab_lo, conv_w,
                                        alog_r, dt_r, seq)
    return _gdn_chunks(q, k, v, z, gc, bc, gt, o_norm.reshape(1, -1), batch, seq)


def _out_proj_ffn(h2d, mixed, w_o, norm_g, w_up, conv_w, conv_b, w_down, final_g, seq, final_norm):
    return _mixer_out_ffn(h2d, mixed, w_o.astype(BF16), norm_g.reshape(1, -1), w_up[:, :D_FF].astype(BF16),
                          w_up[:, D_FF:].astype(BF16), conv_w, conv_b, w_down.astype(BF16),
                          final_g.reshape(1, -1), seq, final_norm)


def kernel(x, rel_bias, attn_norm, attn_w_qkv, attn_w_o, gdn_norm, gdn_w_in, gdn_conv_w, gdn_a_log,
           gdn_dt_bias, gdn_o_norm, gdn_w_o, ffn_norm, ffn_w_up, ffn_conv_w, ffn_conv_b, ffn_w_down,
           final_norm):
    batch, seq, d = x.shape
    h = x.reshape(batch * seq, d)
    mixed = _attention_mixer(h, rel_bias, attn_norm[0], attn_w_qkv[0], batch, seq)
    h = _out_proj_ffn(h, mixed, attn_w_o[0], ffn_norm[0], ffn_w_up[0], ffn_conv_w[0], ffn_conv_b[0],
                      ffn_w_down[0], final_norm, seq, False)
    mixed = _gdn_mixer(h, gdn_norm[0], gdn_w_in[0], gdn_conv_w[0], gdn_a_log[0], gdn_dt_bias[0],
                       gdn_o_norm[0], batch, seq)
    h = _out_proj_ffn(h, mixed, gdn_w_o[0], ffn_norm[1], ffn_w_up[1], ffn_conv_w[1], ffn_conv_b[1],
                      ffn_w_down[1], final_norm, seq, True)
    return h.reshape(batch, seq, d)
```

```python
import functools
import math

import jax
import jax.numpy as jnp
from jax import lax
from jax.experimental import pallas as pl
from jax.experimental.pallas import tpu as pltpu

F32 = jnp.float32
BF16 = jnp.bfloat16

D_MODEL = 1024
HEADS = 8
HEAD_DIM = 128
MOBA_BLOCK = 256
MOBA_TOPK = 3
REL_BUCKETS = 32
REL_MAX_DIST = 1024
GDN_CONV = 4
GDN_CHUNK = 128
D_FF = 2816
FFN_CONV = 3
NORM_EPS = 1e-6
L2_EPS = 1e-6

NEG = -1e30
LOG2E = 1.4426950408889634
N_BIAS_TILES = 6
BF16_SUBLANES = 16
VT_ROWS = HEAD_DIM + BF16_SUBLANES
VMEM_LIMIT = 56 * 1024 * 1024

_NT = (((1,), (1,)), ((), ()))


def _cparams(n_axes):
    return pltpu.CompilerParams(dimension_semantics=("arbitrary",) * n_axes,
                                vmem_limit_bytes=VMEM_LIMIT)


def _rms(x, g):
    return x * lax.rsqrt(jnp.mean(x * x, axis=-1, keepdims=True) + NORM_EPS) * g


def _silu(y):
    h = 0.5 * y
    return h + h * jnp.tanh(h)


def _qkv_kernel(x_ref, g_ref, wqk_ref, wvt_ref, qk_ref, vt_ref):
    xn = _rms(x_ref[...], g_ref[...]).astype(BF16)
    tm = xn.shape[0]
    n_qk = wqk_ref.shape[1]
    for c in range(0, n_qk, 512):
        qk_ref[:, c:c + 512] = jnp.dot(xn, wqk_ref[:, c:c + 512],
                                       preferred_element_type=F32).astype(BF16)
    ones_rows = jnp.where(lax.broadcasted_iota(jnp.int32, (BF16_SUBLANES, MOBA_BLOCK), 0) == 0,
                          1.0, 0.0).astype(BF16)
    vt = lax.dot_general(wvt_ref[...], xn, _NT, preferred_element_type=F32)
    for h in range(HEADS):
        for r in range(tm // MOBA_BLOCK):
            vt_ref[0, h, r, 0:HEAD_DIM, :] = vt[h * HEAD_DIM:(h + 1) * HEAD_DIM,
                                                r * MOBA_BLOCK:(r + 1) * MOBA_BLOCK].astype(BF16)
            vt_ref[0, h, r, HEAD_DIM:VT_ROWS, :] = ones_rows


def _qkv_proj(x2d, g, w_qk, w_vt, batch, seq, tm=512):
    m = x2d.shape[0]
    nb = seq // MOBA_BLOCK
    tps = seq // tm
    rb = tm // MOBA_BLOCK
    return pl.pallas_call(
        _qkv_kernel,
        grid=(m // tm,),
        in_specs=[
            pl.BlockSpec((tm, D_MODEL), lambda i: (i, 0)),
            pl.BlockSpec((1, D_MODEL), lambda i: (0, 0), pipeline_mode=pl.Buffered(1)),
            pl.BlockSpec(w_qk.shape, lambda i: (0, 0), pipeline_mode=pl.Buffered(1)),
            pl.BlockSpec(w_vt.shape, lambda i: (0, 0), pipeline_mode=pl.Buffered(1)),
        ],
        out_specs=[
            pl.BlockSpec((tm, w_qk.shape[1]), lambda i: (i, 0)),
            pl.BlockSpec((1, HEADS, rb, VT_ROWS, MOBA_BLOCK), lambda i: (i // tps, 0, i % tps, 0, 0)),
        ],
        out_shape=[
            jax.ShapeDtypeStruct((m, w_qk.shape[1]), BF16),
            jax.ShapeDtypeStruct((batch, HEADS, nb, VT_ROWS, MOBA_BLOCK), BF16),
        ],
        compiler_params=_cparams(1),
        name="qkv_proj",
    )(x2d, g, w_qk, w_vt)


def _moba_kernel(q_ref, k_ref, vt_ref, bias_ref, o_ref, sel_ref, s_ref, *, nb, unroll, gate_tiles):
    blk = MOBA_BLOCK

    kmean = jnp.sum(k_ref[0].astype(F32).reshape(nb, blk, HEAD_DIM), axis=1) * (1.0 / blk)
    km_hi = kmean.astype(BF16)
    rest = kmean - km_hi.astype(F32)
    km_mid = rest.astype(BF16)
    km_lo = (rest - km_mid.astype(F32)).astype(BF16)
    km3 = jnp.concatenate([km_hi, km_mid, km_lo], axis=0)
    row = lax.broadcasted_iota(jnp.int32, (nb, blk), 0)

    def gates(tt, _):
        for i in range(gate_tiles):
            t = tt * gate_tiles + i
            qt = q_ref[0, pl.ds(pl.multiple_of(t * blk, blk), blk), :]
            g3 = lax.dot_general(km3, qt, _NT, preferred_element_type=F32)
            g = jnp.where(row < t, g3[:nb] + g3[nb:2 * nb] + g3[2 * nb:], NEG)
            sel = jnp.zeros((nb, blk), F32)
            for _ in range(MOBA_TOPK):
                mx = jnp.max(g, axis=0, keepdims=True)
                idx = jnp.min(jnp.where(g == mx, row, nb), axis=0, keepdims=True)
                pick = (row == idx) & (mx > 0.5 * NEG)
                sel = jnp.where(pick, 1.0, sel)
                g = jnp.where(pick, NEG, g)
            sel_ref[t] = jnp.where(row == t, 1.0, sel)
        return 0

    lax.fori_loop(0, nb // gate_tiles, gates, 0)

    qw = 2 * blk

    def fold8(x, op):
        return op(x.reshape(blk // 8, 8, qw), axis=0)

    def block_of(jt, t):
        tc = jnp.minimum(t, 2 * jt + 1)
        return jnp.where(tc == 0, 2 * jt + 1, jnp.where(tc == 1, 2 * jt, tc - 2))

    def n_groups(jt):
        return (2 * jt + 2 + unroll - 1) // unroll

    def scaled_q(jt):
        q = q_ref[0, pl.ds(pl.multiple_of(jt * qw, qw), qw), :]
        return (q.astype(F32) * (HEAD_DIM ** -0.5 * LOG2E)).astype(BF16)

    def score_block(jt, qs, t, slot):
        n = block_of(jt, t)
        kn = k_ref[0, pl.ds(pl.multiple_of(n * blk, blk), blk), :]
        s = lax.dot_general(kn, qs, _NT, preferred_element_type=F32)
        bias, sel = [], []
        for half in range(2):
            jq = 2 * jt + half
            dist = jnp.clip(jq - n, 0, N_BIAS_TILES - 1)
            bias.append(bias_ref[0, dist])
            sel.append(sel_ref[jq, pl.ds(n, 1), :])
        selrow = jnp.where(t <= 2 * jt + 1, jnp.concatenate(sel, axis=1), 0.0)
        s = jnp.where(selrow > 0.0, s + jnp.concatenate(bias, axis=1), NEG)
        s_ref[slot] = s
        return fold8(s, jnp.max)

    def score_group(jt, qs, g, buf):
        gm8 = None
        for i in range(unroll):
            f = score_block(jt, qs, g * unroll + i, buf * unroll + i)
            gm8 = f if gm8 is None else jnp.maximum(gm8, f)
        return gm8

    def absorb_group(jt, g, buf, gm8, state):
        m, acc = state
        m_new = jnp.maximum(m, jnp.max(gm8, axis=0, keepdims=True))
        acc = jnp.exp2(m - m_new) * acc
        for i in range(unroll):
            n = block_of(jt, g * unroll + i)
            p = jnp.exp2(s_ref[buf * unroll + i] - m_new)
            acc = acc + jnp.dot(vt_ref[0, 0, n], p.astype(BF16), preferred_element_type=F32)
        return m_new, acc

    def finish_tile(jt, g, buf, gm8, state):
        _, acc = absorb_group(jt, g, buf, gm8, state)
        out = (acc[:HEAD_DIM] / acc[HEAD_DIM:HEAD_DIM + 1]).T
        o_ref[0, pl.ds(pl.multiple_of(jt * qw, qw), qw), :] = out.astype(o_ref.dtype)

    fresh = (jnp.full((1, qw), NEG, F32), jnp.zeros((VT_ROWS, qw), F32))

    def tile(jt, carry):
        m, acc, gm8, buf = carry
        finish_tile(jt - 1, n_groups(jt - 1) - 1, buf, gm8, (m, acc))
        buf0 = 1 - buf
        qs = scaled_q(jt)
        gm8 = score_group(jt, qs, 0, buf0)

        def body(g, c):
            gm8, state = c
            state = absorb_group(jt, g, (buf0 + g) % 2, gm8, state)
            return score_group(jt, qs, g + 1, (buf0 + g + 1) % 2), state

        gm8, (m, acc) = lax.fori_loop(0, n_groups(jt) - 1, body, (gm8, fresh))
        return m, acc, gm8, (buf0 + n_groups(jt) - 1) % 2

    assert n_groups(0) == 1
    gm8 = score_group(0, scaled_q(0), 0, 0)
    m, acc, gm8, buf = lax.fori_loop(1, nb // 2, tile, (fresh[0], fresh[1], gm8, 0))
    finish_tile(nb // 2 - 1, n_groups(nb // 2 - 1) - 1, buf, gm8, (m, acc))


def _moba_attention(qk3, vt, bias_tiles, batch, seq, unroll=4):
    nb = seq // MOBA_BLOCK
    seq_head = lambda col0: pl.BlockSpec((1, seq, HEAD_DIM), lambda b, h: (b, 0, col0 + h))
    return pl.pallas_call(
        functools.partial(_moba_kernel, nb=nb, unroll=unroll, gate_tiles=4),
        grid=(batch, HEADS),
        in_specs=[
            seq_head(0),
            seq_head(HEADS),
            pl.BlockSpec((1, 1, nb, VT_ROWS, MOBA_BLOCK), lambda b, h: (b, h, 0, 0, 0)),
            pl.BlockSpec((1, N_BIAS_TILES, MOBA_BLOCK, MOBA_BLOCK), lambda b, h: (h, 0, 0, 0)),
        ],
        out_specs=seq_head(0),
        out_shape=jax.ShapeDtypeStruct((batch, seq, HEADS * HEAD_DIM), BF16),
        scratch_shapes=[pltpu.VMEM((nb, nb, MOBA_BLOCK), F32),
                        pltpu.VMEM((2 * unroll, MOBA_BLOCK, 2 * MOBA_BLOCK), F32)],
        compiler_params=_cparams(2),
        name="moba_attention",
    )(qk3, qk3, vt, bias_tiles)


def _rel_bias_tiles(rel_bias):
    max_exact = REL_BUCKETS // 2
    ki = jnp.arange(MOBA_BLOCK)[:, None]
    qi = jnp.arange(MOBA_BLOCK)[None, :]
    dlt = jnp.arange(N_BIAS_TILES)[:, None, None]
    dist = dlt * MOBA_BLOCK + qi - ki
    d = jnp.maximum(dist, 0)
    large = max_exact + (jnp.log(jnp.maximum(d, 1).astype(F32) / max_exact)
                         / math.log(REL_MAX_DIST / max_exact) * (REL_BUCKETS - max_exact)).astype(jnp.int32)
    bucket = jnp.where(d < max_exact, d, jnp.minimum(large, REL_BUCKETS - 1))
    onehot = (bucket[None] == jnp.arange(REL_BUCKETS)[:, None, None, None]).astype(F32)
    tiles = jnp.einsum("hb,btkq->htkq", rel_bias.astype(F32), onehot,
                       precision=lax.Precision.HIGHEST) * LOG2E
    return jnp.where(dist[None] >= 0, tiles, NEG)


def _ffn_kernel(r_ref, a_ref, wo_ref, g_ref, wg_ref, wv_ref, cw_ref, cb_ref, wd_ref, fg_ref, o_ref,
                gbuf_ref, act_ref, *, tiles_per_seq, final_norm, sub):
    i = pl.program_id(0)
    tm = r_ref.shape[0]
    x = r_ref[...] + jnp.dot(a_ref[...], wo_ref[...], preferred_element_type=F32)
    xn = _rms(x, g_ref[...]).astype(BF16)

    @pl.when(i % tiles_per_seq == 0)
    def _():
        gbuf_ref[0:8, :] = jnp.zeros((8, gbuf_ref.shape[1]), F32)

    cw = cw_ref[...]
    cb = cb_ref[...]
    for c0 in range(0, D_FF, sub):
        cs = slice(c0, c0 + sub)
        gate = jnp.dot(xn, wg_ref[:, cs], preferred_element_type=F32)
        val = jnp.dot(xn, wv_ref[:, cs], preferred_element_type=F32)
        gbuf_ref[8:8 + tm, cs] = gate
        y = cb[:, cs] + cw[2:3, cs] * gate
        y = y + cw[1:2, cs] * gbuf_ref[7:7 + tm, cs]
        y = y + cw[0:1, cs] * gbuf_ref[6:6 + tm, cs]
        act_ref[:, cs] = (_silu(y) * val).astype(BF16)
    gbuf_ref[0:8, :] = gbuf_ref[tm:tm + 8, :]
    h = x + jnp.dot(act_ref[...], wd_ref[...], preferred_element_type=F32)
    if final_norm:
        h = _rms(h, fg_ref[...])
    o_ref[...] = h


def _mixer_out_ffn(res, a, w_o, g, w_gate, w_val, conv_w, conv_b, w_down, final_g, seq, final_norm,
                   tm=512, sub=256):
    m = res.shape[0]
    assert D_FF % sub == 0
    tok = lambda i: (i, 0)
    resident = lambda shape: pl.BlockSpec(shape, lambda i: (0, 0), pipeline_mode=pl.Buffered(1))
    return pl.pallas_call(
        functools.partial(_ffn_kernel, tiles_per_seq=seq // tm, final_norm=final_norm, sub=sub),
        grid=(m // tm,),
        in_specs=[
            pl.BlockSpec((tm, D_MODEL), tok),
            pl.BlockSpec((tm, a.shape[1]), tok),
            resident(w_o.shape),
            resident((1, D_MODEL)),
            resident((D_MODEL, D_FF)),
            resident((D_MODEL, D_FF)),
            resident((FFN_CONV, D_FF)),
            resident((1, D_FF)),
            resident((D_FF, D_MODEL)),
            resident((1, D_MODEL)),
        ],
        out_specs=pl.BlockSpec((tm, D_MODEL), tok),
        out_shape=jax.ShapeDtypeStruct((m, D_MODEL), F32),
        scratch_shapes=[
            pltpu.VMEM((tm + 8, D_FF), F32),
            pltpu.VMEM((tm, D_FF), BF16),
        ],
        compiler_params=_cparams(1),
        name="mixer_out_ffn",
    )(res, a, w_o, g, w_gate, w_val, conv_w, conv_b.reshape(1, D_FF), w_down, final_g)


def _softplus(x):
    return jnp.maximum(x, 0.0) + jnp.log1p(jnp.exp(-jnp.abs(x)))


def _chunk_cumsum(x):
    pos = lax.broadcasted_iota(jnp.int32, x.shape, 0) % GDN_CHUNK
    s = 1
    while s < GDN_CHUNK:
        x = x + jnp.where(pos >= s, pltpu.roll(x, s, 0), 0.0)
        s *= 2
    return x


def _gdn_front_kernel(x_ref, g_ref, w_ref, wab_hi_ref, wab_lo_ref, cw_ref, alog_ref, dt_ref,
                      q_ref, k_ref, v_ref, z_ref, gc_ref, bc_ref, gt_ref, xbuf_ref, *, tiles_per_seq, sub):
    i = pl.program_id(0)
    tm = x_ref.shape[0]
    nh = HEADS * HEAD_DIM
    xn = _rms(x_ref[...], g_ref[...])
    xh = xn.astype(BF16)

    @pl.when(i % tiles_per_seq == 0)
    def _():
        xbuf_ref[0:8, :] = jnp.zeros((8, xbuf_ref.shape[1]), F32)

    outs = (q_ref, k_ref, v_ref)

    def conv_heads(c0):
        for s in range(c0 // HEAD_DIM, (c0 + sub) // HEAD_DIM):
            cs = slice(s * HEAD_DIM, (s + 1) * HEAD_DIM)
            cw = cw_ref[:, cs]
            y = cw[3:4, :] * xbuf_ref[8:8 + tm, cs]
            y = y + cw[2:3, :] * xbuf_ref[7:7 + tm, cs]
            y = y + cw[1:2, :] * xbuf_ref[6:6 + tm, cs]
            y = y + cw[0:1, :] * xbuf_ref[5:5 + tm, cs]
            y = _silu(y)
            which, h = divmod(s, HEADS)
            if which < 2:
                y = y * lax.rsqrt(jnp.sum(y * y, axis=-1, keepdims=True) + L2_EPS)
                if which == 0:
                    y = y * (HEAD_DIM ** -0.5)
            outs[which][:, h * HEAD_DIM:(h + 1) * HEAD_DIM] = y.astype(BF16)

    chunks = list(range(0, 3 * nh, sub))
    for idx, c0 in enumerate(chunks):
        xbuf_ref[8:8 + tm, c0:c0 + sub] = jnp.dot(xh, w_ref[:, c0:c0 + sub], preferred_element_type=F32)
        if idx:
            conv_heads(chunks[idx - 1])
    for c0 in range(3 * nh, 4 * nh, sub):
        z_ref[:, c0 - 3 * nh:c0 - 3 * nh + sub] = jnp.dot(
            xh, w_ref[:, c0:c0 + sub], preferred_element_type=F32).astype(BF16)

    xl = (xn - xh.astype(F32)).astype(BF16)
    wh = wab_hi_ref[...]
    ab = (jnp.dot(xh, wh, preferred_element_type=F32)
          + jnp.dot(xh, wab_lo_ref[...], preferred_element_type=F32)
          + jnp.dot(xl, wh, preferred_element_type=F32))
    conv_heads(chunks[-1])
    xbuf_ref[0:8, :] = xbuf_ref[tm:tm + 8, :]

    bc_ref[...] = jax.nn.sigmoid(ab[:, :128])
    g_col = -jnp.exp(alog_ref[...]) * _softplus(ab[:, 128:] + dt_ref[...])
    gc = _chunk_cumsum(g_col)
    gc_ref[...] = gc
    g_row = gc.T
    for r in range(tm // GDN_CHUNK):
        gt_ref[r] = g_row[0:HEADS, r * GDN_CHUNK:(r + 1) * GDN_CHUNK]


def _gdn_front(x2d, g, w_main, w_ab_hi, w_ab_lo, conv_w, alog_r, dt_r, seq, tm=512, sub=256):
    m = x2d.shape[0]
    nh = HEADS * HEAD_DIM
    tok = lambda i: (i, 0)
    resident = lambda arr: pl.BlockSpec(arr.shape, lambda i: (0, 0), pipeline_mode=pl.Buffered(1))
    return pl.pallas_call(
        functools.partial(_gdn_front_kernel, tiles_per_seq=seq // tm, sub=sub),
        grid=(m // tm,),
        in_specs=[pl.BlockSpec((tm, D_MODEL), tok)]
        + [resident(a) for a in (g, w_main, w_ab_hi, w_ab_lo, conv_w, alog_r, dt_r)],
        out_specs=[
            pl.BlockSpec((tm, nh), tok),
            pl.BlockSpec((tm, nh), tok),
            pl.BlockSpec((tm, nh), tok),
            pl.BlockSpec((tm, nh), tok),
            pl.BlockSpec((tm, 128), tok),
            pl.BlockSpec((tm, 128), tok),
            pl.BlockSpec((tm // GDN_CHUNK, HEADS, GDN_CHUNK), lambda i: (i, 0, 0)),
        ],
        out_shape=[
            jax.ShapeDtypeStruct((m, nh), BF16),
            jax.ShapeDtypeStruct((m, nh), BF16),
            jax.ShapeDtypeStruct((m, nh), BF16),
            jax.ShapeDtypeStruct((m, nh), BF16),
            jax.ShapeDtypeStruct((m, 128), F32),
            jax.ShapeDtypeStruct((m, 128), F32),
            jax.ShapeDtypeStruct((m // GDN_CHUNK, HEADS, GDN_CHUNK), F32),
        ],
        scratch_shapes=[pltpu.VMEM((tm + 8, 3 * nh), F32)],
        compiler_params=_cparams(1),
        name="gdn_front",
    )(x2d, g, w_main, w_ab_hi, w_ab_lo, conv_w, alog_r, dt_r)


def _inverse_masks(n):
    ri = lax.broadcasted_iota(jnp.int32, (n, n), 0)
    ci = lax.broadcasted_iota(jnp.int32, (n, n), 1)
    masks = []
    s = 1
    while s < n:
        masks.append((ri // (2 * s) == ci // (2 * s)) & ((ri // s) % 2 == 1) & ((ci // s) % 2 == 0))
        s *= 2
    return ri == ci, masks


def _unit_lower_inverses(lows, eye, masks):
    ts = [jnp.where(eye, 1.0, 0.0) - jnp.where(masks[0], low, 0.0) for low in lows]
    for off in masks[1:]:
        tbs = [t.astype(BF16) for t in ts]
        lts = [jnp.dot(jnp.where(off, low, 0.0).astype(BF16), tb, preferred_element_type=F32)
               for low, tb in zip(lows, tbs)]
        ts = [t - jnp.dot(tb, lt.astype(BF16), preferred_element_type=F32)
              for t, tb, lt in zip(ts, tbs, lts)]
    return ts


def _gdn_chunk_kernel(q_ref, k_ref, v_ref, z_ref, gc_ref, bc_ref, gt_ref, on_ref, o_ref, s_ref, *, group):
    c = GDN_CHUNK
    n_chunks = q_ref.shape[0] // c
    hs = [slice(h * HEAD_DIM, (h + 1) * HEAD_DIM) for h in range(HEADS)]

    @pl.when(pl.program_id(1) == 0)
    def _():
        s_ref[...] = jnp.zeros_like(s_ref)

    ri = lax.broadcasted_iota(jnp.int32, (c, c), 0)
    ci = lax.broadcasted_iota(jnp.int32, (c, c), 1)
    causal = ri >= ci
    strict = ri > ci
    eye, masks = _inverse_masks(c)
    onorm = on_ref[...]

    def chunks(ig, _):
        rows, gcol, bcol, gt_rows, items = [], [], [], [], []
        for j in range(group):
            ic = ig * group + j
            rows.append(pl.ds(pl.multiple_of(ic * c, c), c))
            gc_t = gc_ref[rows[j], :]
            bc_t = bc_ref[rows[j], :]
            gt_t = gt_ref[ic]
            for h in range(HEADS):
                items.append((j, h))
                gcol.append(gc_t[:, h:h + 1])
                bcol.append(bc_t[:, h:h + 1])
                gt_rows.append(gt_t[h:h + 1, :])
        n = range(len(items))
        q = [q_ref[rows[j], hs[h]] for j, h in items]
        k = [k_ref[rows[j], hs[h]] for j, h in items]
        glast = [g[c - 1:c, :] for g in gcol]
        egcol = [jnp.exp(g) for g in gcol]
        decay = [jnp.exp(jnp.where(causal, gcol[i] - gt_rows[i], NEG)) for i in n]
        kq = [lax.dot_general(jnp.concatenate([k[i], q[i]], axis=0), k[i], _NT, preferred_element_type=F32)
              for i in n]
        low = [jnp.where(strict, bcol[i] * kq[i][:c] * decay[i], 0.0) for i in n]
        t = _unit_lower_inverses(low, eye, masks)
        kf = [x.astype(F32) for x in k]
        rhs = [jnp.concatenate([v_ref[rows[j], hs[h]].astype(F32) * bcol[i], kf[i] * (bcol[i] * egcol[i])],
                               axis=1).astype(BF16) for i, (j, h) in enumerate(items)]
        sol = [jnp.dot(t[i].astype(BF16), rhs[i], preferred_element_type=F32) for i in n]
        attn = [jnp.where(causal, kq[i][c:] * decay[i], 0.0) for i in n]
        kdt = [(kf[i] * jnp.exp(glast[i] - gcol[i])).T for i in n]
        wq = [jnp.concatenate([sol[i][:, HEAD_DIM:], q[i].astype(F32) * egcol[i]], axis=0).astype(BF16)
              for i in n]
        ak = [jnp.concatenate([attn[i], kdt[i]], axis=0).astype(BF16) for i in n]
        state = [s_ref[h] for h in range(HEADS)]
        for j in range(group):
            idx = [j * HEADS + h for h in range(HEADS)]
            ws = [jnp.dot(wq[i], state[h].astype(BF16), preferred_element_type=F32)
                  for h, i in enumerate(idx)]
            v_new = [sol[i][:, :HEAD_DIM] - ws[h][:c] for h, i in enumerate(idx)]
            r2 = [jnp.dot(ak[i], v_new[h].astype(BF16), preferred_element_type=F32)
                  for h, i in enumerate(idx)]
            state = [state[h] * jnp.exp(glast[i]) + r2[h][c:] for h, i in enumerate(idx)]
            for h in range(HEADS):
                z = z_ref[rows[j], hs[h]].astype(F32)
                o = _rms(ws[h][c:] + r2[h][:c], onorm) * _silu(z)
                o_ref[rows[j], hs[h]] = o.astype(o_ref.dtype)
        for h in range(HEADS):
            s_ref[h] = state[h]
        return 0

    lax.fori_loop(0, n_chunks // group, chunks, 0)


def _gdn_chunks(q, k, v, z, gc, bc, gt, o_norm, batch, seq, ts=1024, group=2):
    m = q.shape[0]
    nh = HEADS * HEAD_DIM
    tps = seq // ts
    assert (ts // GDN_CHUNK) % group == 0
    tok = lambda b, t: (b * tps + t, 0)
    return pl.pallas_call(
        functools.partial(_gdn_chunk_kernel, group=group),
        grid=(batch, tps),
        in_specs=[
            pl.BlockSpec((ts, nh), tok),
            pl.BlockSpec((ts, nh), tok),
            pl.BlockSpec((ts, nh), tok),
            pl.BlockSpec((ts, nh), tok),
            pl.BlockSpec((ts, 128), tok),
            pl.BlockSpec((ts, 128), tok),
            pl.BlockSpec((ts // GDN_CHUNK, HEADS, GDN_CHUNK), lambda b, t: (b * tps + t, 0, 0)),
            pl.BlockSpec((1, HEAD_DIM), lambda b, t: (0, 0)),
        ],
        out_specs=pl.BlockSpec((ts, nh), tok),
        out_shape=jax.ShapeDtypeStruct((m, nh), BF16),
        scratch_shapes=[pltpu.VMEM((HEADS, HEAD_DIM, HEAD_DIM), F32)],
        compiler_params=_cparams(2),
        name="gdn_chunks",
    )(q, k, v, z, gc, bc, gt, o_norm)


def _attention_mixer(h2d, rel_bias, norm_g, w_qkv, batch, seq):
    nh = HEADS * HEAD_DIM
    w_qk = w_qkv[:, :2 * nh].astype(BF16)
    w_vt = w_qkv[:, 2 * nh:].T.astype(BF16)
    qk, vt = _qkv_proj(h2d, norm_g.reshape(1, -1), w_qk, w_vt, batch, seq)
    o = _moba_attention(qk.reshape(batch, seq, 2 * nh), vt, _rel_bias_tiles(rel_bias), batch, seq)
    return o.reshape(batch * seq, nh)


def _gdn_mixer(h2d, norm_g, w_in, conv_w, a_log, dt_bias, o_norm, batch, seq):
    nh = HEADS * HEAD_DIM
    w_main = w_in[:, :4 * nh].astype(BF16)
    w_b = w_in[:, 4 * nh:4 * nh + HEADS]
    w_a = w_in[:, 4 * nh + HEADS:]
    pad = jnp.zeros((D_MODEL, 128 - HEADS), F32)
    w_ab = jnp.concatenate([w_b, pad, w_a, pad], axis=1)
    w_ab_hi = w_ab.astype(BF16)
    w_ab_lo = (w_ab - w_ab_hi.astype(F32)).astype(BF16)
    pad_r = jnp.zeros((128 - HEADS,), F32)
    alog_r = jnp.concatenate([a_log, pad_r]).reshape(1, 128)
    dt_r = jnp.concatenate([dt_bias, pad_r]).reshape(1, 128)
    q, k, v, z, gc, bc, gt = _gdn_front(h2d, norm_g.reshape(1, -1), w_main, w_ab_hi, w_ab_lo, conv_w,
                                        alog_r, dt_r, seq)
    return _gdn_chunks(q, k, v, z, gc, bc, gt, o_norm.reshape(1, -1), batch, seq)


def _out_proj_ffn(h2d, mixed, w_o, norm_g, w_up, conv_w, conv_b, w_down, final_g, seq, final_norm):
    return _mixer_out_ffn(h2d, mixed, w_o.astype(BF16), norm_g.reshape(1, -1), w_up[:, :D_FF].astype(BF16),
                          w_up[:, D_FF:].astype(BF16), conv_w, conv_b, w_down.astype(BF16),
                          final_g.reshape(1, -1), seq, final_norm)


def kernel(x, rel_bias, attn_norm, attn_w_qkv, attn_w_o, gdn_norm, gdn_w_in, gdn_conv_w, gdn_a_log,
           gdn_dt_bias, gdn_o_norm, gdn_w_o, ffn_norm, ffn_w_up, ffn_conv_w, ffn_conv_b, ffn_w_down,
           final_norm):
    batch, seq, d = x.shape
    h = x.reshape(batch * seq, d)
    mixed = _attention_mixer(h, rel_bias, attn_norm[0], attn_w_qkv[0], batch, seq)
    h = _out_proj_ffn(h, mixed, attn_w_o[0], ffn_norm[0], ffn_w_up[0], ffn_conv_w[0], ffn_conv_b[0],
                      ffn_w_down[0], final_norm, seq, False)
    mixed = _gdn_mixer(h, gdn_norm[0], gdn_w_in[0], gdn_conv_w[0], gdn_a_log[0], gdn_dt_bias[0],
                       gdn_o_norm[0], batch, seq)
    h = _out_proj_ffn(h, mixed, gdn_w_o[0], ffn_norm[1], ffn_w_up[1], ffn_conv_w[1], ffn_conv_b[1],
                      ffn_w_down[1], final_norm, seq, True)
    return h.reshape(batch, seq, d)
```

```python
import functools
import math

import jax
import jax.numpy as jnp
from jax import lax
from jax.experimental import pallas as pl
from jax.experimental.pallas import tpu as pltpu

F32 = jnp.float32
BF16 = jnp.bfloat16

D_MODEL = 1024
HEADS = 8
HEAD_DIM = 128
MOBA_BLOCK = 256
MOBA_TOPK = 3
REL_BUCKETS = 32
REL_MAX_DIST = 1024
GDN_CONV = 4
GDN_CHUNK = 128
D_FF = 2816
FFN_CONV = 3
NORM_EPS = 1e-6
L2_EPS = 1e-6

NEG = -1e30
LOG2E = 1.4426950408889634
N_BIAS_TILES = 6
BF16_SUBLANES = 16
VT_ROWS = HEAD_DIM + BF16_SUBLANES
VMEM_LIMIT = 56 * 1024 * 1024

_NT = (((1,), (1,)), ((), ()))


def _cparams(n_axes):
    return pltpu.CompilerParams(dimension_semantics=("arbitrary",) * n_axes,
                                vmem_limit_bytes=VMEM_LIMIT)


def _rms(x, g):
    return x * lax.rsqrt(jnp.mean(x * x, axis=-1, keepdims=True) + NORM_EPS) * g


def _silu(y):
    h = 0.5 * y
    return h + h * jnp.tanh(h)


def _qkv_kernel(x_ref, g_ref, wqk_ref, wvt_ref, qk_ref, vt_ref):
    xn = _rms(x_ref[...], g_ref[...]).astype(BF16)
    tm = xn.shape[0]
    n_qk = wqk_ref.shape[1]
    for c in range(0, n_qk, 512):
        qk_ref[:, c:c + 512] = jnp.dot(xn, wqk_ref[:, c:c + 512],
                                       preferred_element_type=F32).astype(BF16)
    ones_rows = jnp.where(lax.broadcasted_iota(jnp.int32, (BF16_SUBLANES, MOBA_BLOCK), 0) == 0,
                          1.0, 0.0).astype(BF16)
    vt = lax.dot_general(wvt_ref[...], xn, _NT, preferred_element_type=F32)
    for h in range(HEADS):
        for r in range(tm // MOBA_BLOCK):
            vt_ref[0, h, r, 0:HEAD_DIM, :] = vt[h * HEAD_DIM:(h + 1) * HEAD_DIM,
                                                r * MOBA_BLOCK:(r + 1) * MOBA_BLOCK].astype(BF16)
            vt_ref[0, h, r, HEAD_DIM:VT_ROWS, :] = ones_rows


def _qkv_proj(x2d, g, w_qk, w_vt, batch, seq, tm=512):
    m = x2d.shape[0]
    nb = seq // MOBA_BLOCK
    tps = seq // tm
    rb = tm // MOBA_BLOCK
    return pl.pallas_call(
        _qkv_kernel,
        grid=(m // tm,),
        in_specs=[
            pl.BlockSpec((tm, D_MODEL), lambda i: (i, 0)),
            pl.BlockSpec((1, D_MODEL), lambda i: (0, 0), pipeline_mode=pl.Buffered(1)),
            pl.BlockSpec(w_qk.shape, lambda i: (0, 0), pipeline_mode=pl.Buffered(1)),
            pl.BlockSpec(w_vt.shape, lambda i: (0, 0), pipeline_mode=pl.Buffered(1)),
        ],
        out_specs=[
            pl.BlockSpec((tm, w_qk.shape[1]), lambda i: (i, 0)),
            pl.BlockSpec((1, HEADS, rb, VT_ROWS, MOBA_BLOCK), lambda i: (i // tps, 0, i % tps, 0, 0)),
        ],
        out_shape=[
            jax.ShapeDtypeStruct((m, w_qk.shape[1]), BF16),
            jax.ShapeDtypeStruct((batch, HEADS, nb, VT_ROWS, MOBA_BLOCK), BF16),
        ],
        compiler_params=_cparams(1),
        name="qkv_proj",
    )(x2d, g, w_qk, w_vt)


def _moba_kernel(q_ref, k_ref, vt_ref, bias_ref, o_ref, sel_ref, s_ref, *, nb, unroll, gate_tiles):
    blk = MOBA_BLOCK

    kmean = jnp.sum(k_ref[0].astype(F32).reshape(nb, blk, HEAD_DIM), axis=1) * (1.0 / blk)
    km_hi = kmean.astype(BF16)
    rest = kmean - km_hi.astype(F32)
    km_mid = rest.astype(BF16)
    km_lo = (rest - km_mid.astype(F32)).astype(BF16)
    km3 = jnp.concatenate([km_hi, km_mid, km_lo], axis=0)
    row = lax.broadcasted_iota(jnp.int32, (nb, blk), 0)

    def gates(tt, _):
        for i in range(gate_tiles):
            t = tt * gate_tiles + i
            qt = q_ref[0, pl.ds(pl.multiple_of(t * blk, blk), blk), :]
            g3 = lax.dot_general(km3, qt, _NT, preferred_element_type=F32)
            g = jnp.where(row < t, g3[:nb] + g3[nb:2 * nb] + g3[2 * nb:], NEG)
            sel = jnp.zeros((nb, blk), F32)
            for _ in range(MOBA_TOPK):
                mx = jnp.max(g, axis=0, keepdims=True)
                idx = jnp.min(jnp.where(g == mx, row, nb), axis=0, keepdims=True)
                pick = (row == idx) & (mx > 0.5 * NEG)
                sel = jnp.where(pick, 1.0, sel)
                g = jnp.where(pick, NEG, g)
            sel_ref[t] = jnp.where(row == t, 1.0, sel)
        return 0

    lax.fori_loop(0, nb // gate_tiles, gates, 0)

    qw = 2 * blk

    def fold8(x, op):
        return op(x.reshape(blk // 8, 8, qw), axis=0)

    def block_of(jt, t):
        tc = jnp.minimum(t, 2 * jt + 1)
        return jnp.where(tc == 0, 2 * jt + 1, jnp.where(tc == 1, 2 * jt, tc - 2))

    def n_groups(jt):
        return (2 * jt + 2 + unroll - 1) // unroll

    def scaled_q(jt):
        q = q_ref[0, pl.ds(pl.multiple_of(jt * qw, qw), qw), :]
        return (q.astype(F32) * (HEAD_DIM ** -0.5 * LOG2E)).astype(BF16)

    def score_block(jt, qs, t, slot):
        n = block_of(jt, t)
        kn = k_ref[0, pl.ds(pl.multiple_of(n * blk, blk), blk), :]
        s = lax.dot_general(kn, qs, _NT, preferred_element_type=F32)
        bias, sel = [], []
        for half in range(2):
            jq = 2 * jt + half
            dist = jnp.clip(jq - n, 0, N_BIAS_TILES - 1)
            bias.append(bias_ref[0, dist])
            sel.append(sel_ref[jq, pl.ds(n, 1), :])
        selrow = jnp.where(t <= 2 * jt + 1, jnp.concatenate(sel, axis=1), 0.0)
        s = jnp.where(selrow > 0.0, s + jnp.concatenate(bias, axis=1), NEG)
        s_ref[slot] = s
        return fold8(s, jnp.max)

    def score_group(jt, qs, g, buf):
        gm8 = None
        for i in range(unroll):
            f = score_block(jt, qs, g * unroll + i, buf * unroll + i)
            gm8 = f if gm8 is None else jnp.maximum(gm8, f)
        return gm8

    def absorb_group(jt, g, buf, gm8, state):
        m, acc = state
        m_new = jnp.maximum(m, jnp.max(gm8, axis=0, keepdims=True))
        acc = jnp.exp2(m - m_new) * acc
        for i in range(unroll):
            n = block_of(jt, g * unroll + i)
            p = jnp.exp2(s_ref[buf * unroll + i] - m_new)
            acc = acc + jnp.dot(vt_ref[0, 0, n], p.astype(BF16), preferred_element_type=F32)
        return m_new, acc

    fresh = (jnp.full((1, qw), NEG, F32), jnp.zeros((VT_ROWS, qw), F32))

    def tile(jt, _):
        qs = scaled_q(jt)
        groups = n_groups(jt)
        gm8 = score_group(jt, qs, 0, 0)

        def two_trips(i, c):
            gm8_even, state = c
            g = 2 * i
            gm8_odd = score_group(jt, qs, g + 1, 1)
            state = absorb_group(jt, g, 0, gm8_even, state)
            gm8_even = score_group(jt, qs, g + 2, 0)
            state = absorb_group(jt, g + 1, 1, gm8_odd, state)
            return gm8_even, state

        gm8, state = lax.fori_loop(0, (groups - 1) // 2, two_trips, (gm8, fresh))

        def last_is_odd(gm8_even, state):
            gm8_odd = score_group(jt, qs, groups - 1, 1)
            state = absorb_group(jt, groups - 2, 0, gm8_even, state)
            return absorb_group(jt, groups - 1, 1, gm8_odd, state)

        def last_is_even(gm8_even, state):
            return absorb_group(jt, groups - 1, 0, gm8_even, state)

        _, acc = lax.cond(groups % 2 == 0, last_is_odd, last_is_even, gm8, state)
        out = (acc[:HEAD_DIM] / acc[HEAD_DIM:HEAD_DIM + 1]).T
        o_ref[0, pl.ds(pl.multiple_of(jt * qw, qw), qw), :] = out.astype(o_ref.dtype)
        return 0

    lax.fori_loop(0, nb // 2, tile, 0)


def _moba_attention(qk3, vt, bias_tiles, batch, seq, unroll=4):
    nb = seq // MOBA_BLOCK
    seq_head = lambda col0: pl.BlockSpec((1, seq, HEAD_DIM), lambda b, h: (b, 0, col0 + h))
    return pl.pallas_call(
        functools.partial(_moba_kernel, nb=nb, unroll=unroll, gate_tiles=4),
        grid=(batch, HEADS),
        in_specs=[
            seq_head(0),
            seq_head(HEADS),
            pl.BlockSpec((1, 1, nb, VT_ROWS, MOBA_BLOCK), lambda b, h: (b, h, 0, 0, 0)),
            pl.BlockSpec((1, N_BIAS_TILES, MOBA_BLOCK, MOBA_BLOCK), lambda b, h: (h, 0, 0, 0)),
        ],
        out_specs=seq_head(0),
        out_shape=jax.ShapeDtypeStruct((batch, seq, HEADS * HEAD_DIM), BF16),
        scratch_shapes=[pltpu.VMEM((nb, nb, MOBA_BLOCK), F32),
                        pltpu.VMEM((2 * unroll, MOBA_BLOCK, 2 * MOBA_BLOCK), F32)],
        compiler_params=_cparams(2),
        name="moba_attention",
    )(qk3, qk3, vt, bias_tiles)


def _rel_bias_tiles(rel_bias):
    max_exact = REL_BUCKETS // 2
    ki = jnp.arange(MOBA_BLOCK)[:, None]
    qi = jnp.arange(MOBA_BLOCK)[None, :]
    dlt = jnp.arange(N_BIAS_TILES)[:, None, None]
    dist = dlt * MOBA_BLOCK + qi - ki
    d = jnp.maximum(dist, 0)
    large = max_exact + (jnp.log(jnp.maximum(d, 1).astype(F32) / max_exact)
                         / math.log(REL_MAX_DIST / max_exact) * (REL_BUCKETS - max_exact)).astype(jnp.int32)
    bucket = jnp.where(d < max_exact, d, jnp.minimum(large, REL_BUCKETS - 1))
    onehot = (bucket[None] == jnp.arange(REL_BUCKETS)[:, None, None, None]).astype(F32)
    tiles = jnp.einsum("hb,btkq->htkq", rel_bias.astype(F32), onehot,
                       precision=lax.Precision.HIGHEST) * LOG2E
    return jnp.where(dist[None] >= 0, tiles, NEG)


def _ffn_kernel(r_ref, a_ref, wo_ref, g_ref, wg_ref, wv_ref, cw_ref, cb_ref, wd_ref, fg_ref, o_ref,
                gbuf_ref, act_ref, *, tiles_per_seq, final_norm, sub):
    i = pl.program_id(0)
    tm = r_ref.shape[0]
    x = r_ref[...] + jnp.dot(a_ref[...], wo_ref[...], preferred_element_type=F32)
    xn = _rms(x, g_ref[...]).astype(BF16)

    @pl.when(i % tiles_per_seq == 0)
    def _():
        gbuf_ref[0:8, :] = jnp.zeros((8, gbuf_ref.shape[1]), F32)

    cw = cw_ref[...]
    cb = cb_ref[...]
    for c0 in range(0, D_FF, sub):
        cs = slice(c0, c0 + sub)
        gate = jnp.dot(xn, wg_ref[:, cs], preferred_element_type=F32)
        val = jnp.dot(xn, wv_ref[:, cs], preferred_element_type=F32)
        gbuf_ref[8:8 + tm, cs] = gate
        y = cb[:, cs] + cw[2:3, cs] * gate
        y = y + cw[1:2, cs] * gbuf_ref[7:7 + tm, cs]
        y = y + cw[0:1, cs] * gbuf_ref[6:6 + tm, cs]
        act_ref[:, cs] = (_silu(y) * val).astype(BF16)
    gbuf_ref[0:8, :] = gbuf_ref[tm:tm + 8, :]
    h = x + jnp.dot(act_ref[...], wd_ref[...], preferred_element_type=F32)
    if final_norm:
        h = _rms(h, fg_ref[...])
    o_ref[...] = h


def _mixer_out_ffn(res, a, w_o, g, w_gate, w_val, conv_w, conv_b, w_down, final_g, seq, final_norm,
                   tm=512, sub=256):
    m = res.shape[0]
    assert D_FF % sub == 0
    tok = lambda i: (i, 0)
    resident = lambda shape: pl.BlockSpec(shape, lambda i: (0, 0), pipeline_mode=pl.Buffered(1))
    return pl.pallas_call(
        functools.partial(_ffn_kernel, tiles_per_seq=seq // tm, final_norm=final_norm, sub=sub),
        grid=(m // tm,),
        in_specs=[
            pl.BlockSpec((tm, D_MODEL), tok),
            pl.BlockSpec((tm, a.shape[1]), tok),
            resident(w_o.shape),
            resident((1, D_MODEL)),
            resident((D_MODEL, D_FF)),
            resident((D_MODEL, D_FF)),
            resident((FFN_CONV, D_FF)),
            resident((1, D_FF)),
            resident((D_FF, D_MODEL)),
            resident((1, D_MODEL)),
        ],
        out_specs=pl.BlockSpec((tm, D_MODEL), tok),
        out_shape=jax.ShapeDtypeStruct((m, D_MODEL), F32),
        scratch_shapes=[
            pltpu.VMEM((tm + 8, D_FF), F32),
            pltpu.VMEM((tm, D_FF), BF16),
        ],
        compiler_params=_cparams(1),
        name="mixer_out_ffn",
    )(res, a, w_o, g, w_gate, w_val, conv_w, conv_b.reshape(1, D_FF), w_down, final_g)


def _softplus(x):
    return jnp.maximum(x, 0.0) + jnp.log1p(jnp.exp(-jnp.abs(x)))


def _chunk_cumsum(x):
    pos = lax.broadcasted_iota(jnp.int32, x.shape, 0) % GDN_CHUNK
    s = 1
    while s < GDN_CHUNK:
        x = x + jnp.where(pos >= s, pltpu.roll(x, s, 0), 0.0)
        s *= 2
    return x


def _gdn_front_kernel(x_ref, g_ref, w_ref, wab_hi_ref, wab_lo_ref, cw_ref, alog_ref, dt_ref,
                      q_ref, k_ref, v_ref, z_ref, gc_ref, bc_ref, gt_ref, xbuf_ref, *, tiles_per_seq, sub):
    i = pl.program_id(0)
    tm = x_ref.shape[0]
    nh = HEADS * HEAD_DIM
    xn = _rms(x_ref[...], g_ref[...])
    xh = xn.astype(BF16)

    @pl.when(i % tiles_per_seq == 0)
    def _():
        xbuf_ref[0:8, :] = jnp.zeros((8, xbuf_ref.shape[1]), F32)

    outs = (q_ref, k_ref, v_ref)

    def conv_heads(c0):
        for s in range(c0 // HEAD_DIM, (c0 + sub) // HEAD_DIM):
            cs = slice(s * HEAD_DIM, (s + 1) * HEAD_DIM)
            cw = cw_ref[:, cs]
            y = cw[3:4, :] * xbuf_ref[8:8 + tm, cs]
            y = y + cw[2:3, :] * xbuf_ref[7:7 + tm, cs]
            y = y + cw[1:2, :] * xbuf_ref[6:6 + tm, cs]
            y = y + cw[0:1, :] * xbuf_ref[5:5 + tm, cs]
            y = _silu(y)
            which, h = divmod(s, HEADS)
            if which < 2:
                y = y * lax.rsqrt(jnp.sum(y * y, axis=-1, keepdims=True) + L2_EPS)
                if which == 0:
                    y = y * (HEAD_DIM ** -0.5)
            outs[which][:, h * HEAD_DIM:(h + 1) * HEAD_DIM] = y.astype(BF16)

    chunks = list(range(0, 3 * nh, sub))
    for idx, c0 in enumerate(chunks):
        xbuf_ref[8:8 + tm, c0:c0 + sub] = jnp.dot(xh, w_ref[:, c0:c0 + sub], preferred_element_type=F32)
        if idx:
            conv_heads(chunks[idx - 1])
    for c0 in range(3 * nh, 4 * nh, sub):
        z_ref[:, c0 - 3 * nh:c0 - 3 * nh + sub] = jnp.dot(
            xh, w_ref[:, c0:c0 + sub], preferred_element_type=F32).astype(BF16)

    xl = (xn - xh.astype(F32)).astype(BF16)
    wh = wab_hi_ref[...]
    ab = (jnp.dot(xh, wh, preferred_element_type=F32)
          + jnp.dot(xh, wab_lo_ref[...], preferred_element_type=F32)
          + jnp.dot(xl, wh, preferred_element_type=F32))
    conv_heads(chunks[-1])
    xbuf_ref[0:8, :] = xbuf_ref[tm:tm + 8, :]

    bc_ref[...] = jax.nn.sigmoid(ab[:, :128])
    g_col = -jnp.exp(alog_ref[...]) * _softplus(ab[:, 128:] + dt_ref[...])
    gc = _chunk_cumsum(g_col)
    gc_ref[...] = gc
    g_row = gc.T
    for r in range(tm // GDN_CHUNK):
        gt_ref[r] = g_row[0:HEADS, r * GDN_CHUNK:(r + 1) * GDN_CHUNK]


def _gdn_front(x2d, g, w_main, w_ab_hi, w_ab_lo, conv_w, alog_r, dt_r, seq, tm=512, sub=256):
    m = x2d.shape[0]
    nh = HEADS * HEAD_DIM
    tok = lambda i: (i, 0)
    resident = lambda arr: pl.BlockSpec(arr.shape, lambda i: (0, 0), pipeline_mode=pl.Buffered(1))
    return pl.pallas_call(
        functools.partial(_gdn_front_kernel, tiles_per_seq=seq // tm, sub=sub),
        grid=(m // tm,),
        in_specs=[pl.BlockSpec((tm, D_MODEL), tok)]
        + [resident(a) for a in (g, w_main, w_ab_hi, w_ab_lo, conv_w, alog_r, dt_r)],
        out_specs=[
            pl.BlockSpec((tm, nh), tok),
            pl.BlockSpec((tm, nh), tok),
            pl.BlockSpec((tm, nh), tok),
            pl.BlockSpec((tm, nh), tok),
            pl.BlockSpec((tm, 128), tok),
            pl.BlockSpec((tm, 128), tok),
            pl.BlockSpec((tm // GDN_CHUNK, HEADS, GDN_CHUNK), lambda i: (i, 0, 0)),
        ],
        out_shape=[
            jax.ShapeDtypeStruct((m, nh), BF16),
            jax.ShapeDtypeStruct((m, nh), BF16),
            jax.ShapeDtypeStruct((m, nh), BF16),
            jax.ShapeDtypeStruct((m, nh), BF16),
            jax.ShapeDtypeStruct((m, 128), F32),
            jax.ShapeDtypeStruct((m, 128), F32),
            jax.ShapeDtypeStruct((m // GDN_CHUNK, HEADS, GDN_CHUNK), F32),
        ],
        scratch_shapes=[pltpu.VMEM((tm + 8, 3 * nh), F32)],
        compiler_params=_cparams(1),
        name="gdn_front",
    )(x2d, g, w_main, w_ab_hi, w_ab_lo, conv_w, alog_r, dt_r)


def _inverse_masks(n):
    ri = lax.broadcasted_iota(jnp.int32, (n, n), 0)
    ci = lax.broadcasted_iota(jnp.int32, (n, n), 1)
    masks = []
    s = 1
    while s < n:
        masks.append((ri // (2 * s) == ci // (2 * s)) & ((ri // s) % 2 == 1) & ((ci // s) % 2 == 0))
        s *= 2
    return ri == ci, masks


def _unit_lower_inverses(lows, eye, masks):
    ts = [jnp.where(eye, 1.0, 0.0) - jnp.where(masks[0], low, 0.0) for low in lows]
    for off in masks[1:]:
        tbs = [t.astype(BF16) for t in ts]
        lts = [jnp.dot(jnp.where(off, low, 0.0).astype(BF16), tb, preferred_element_type=F32)
               for low, tb in zip(lows, tbs)]
        ts = [t - jnp.dot(tb, lt.astype(BF16), preferred_element_type=F32)
              for t, tb, lt in zip(ts, tbs, lts)]
    return ts


def _gdn_chunk_kernel(q_ref, k_ref, v_ref, z_ref, gc_ref, bc_ref, gt_ref, on_ref, o_ref, s_ref, *, group):
    c = GDN_CHUNK
    n_chunks = q_ref.shape[0] // c
    hs = [slice(h * HEAD_DIM, (h + 1) * HEAD_DIM) for h in range(HEADS)]

    @pl.when(pl.program_id(1) == 0)
    def _():
        s_ref[...] = jnp.zeros_like(s_ref)

    ri = lax.broadcasted_iota(jnp.int32, (c, c), 0)
    ci = lax.broadcasted_iota(jnp.int32, (c, c), 1)
    causal = ri >= ci
    strict = ri > ci
    eye, masks = _inverse_masks(c)
    onorm = on_ref[...]

    def chunks(ig, _):
        rows, gcol, bcol, gt_rows, items = [], [], [], [], []
        for j in range(group):
            ic = ig * group + j
            rows.append(pl.ds(pl.multiple_of(ic * c, c), c))
            gc_t = gc_ref[rows[j], :]
            bc_t = bc_ref[rows[j], :]
            gt_t = gt_ref[ic]
            for h in range(HEADS):
                items.append((j, h))
                gcol.append(gc_t[:, h:h + 1])
                bcol.append(bc_t[:, h:h + 1])
                gt_rows.append(gt_t[h:h + 1, :])
        n = range(len(items))
        q = [q_ref[rows[j], hs[h]] for j, h in items]
        k = [k_ref[rows[j], hs[h]] for j, h in items]
        glast = [g[c - 1:c, :] for g in gcol]
        egcol = [jnp.exp(g) for g in gcol]
        decay = [jnp.exp(jnp.where(causal, gcol[i] - gt_rows[i], NEG)) for i in n]
        kq = [lax.dot_general(jnp.concatenate([k[i], q[i]], axis=0), k[i], _NT, preferred_element_type=F32)
              for i in n]
        low = [jnp.where(strict, bcol[i] * kq[i][:c] * decay[i], 0.0) for i in n]
        t = _unit_lower_inverses(low, eye, masks)
        kf = [x.astype(F32) for x in k]
        rhs = [jnp.concatenate([v_ref[rows[j], hs[h]].astype(F32) * bcol[i], kf[i] * (bcol[i] * egcol[i])],
                               axis=1).astype(BF16) for i, (j, h) in enumerate(items)]
        sol = [jnp.dot(t[i].astype(BF16), rhs[i], preferred_element_type=F32) for i in n]
        attn = [jnp.where(causal, kq[i][c:] * decay[i], 0.0) for i in n]
        kdt = [(kf[i] * jnp.exp(glast[i] - gcol[i])).T for i in n]
        wq = [jnp.concatenate([sol[i][:, HEAD_DIM:], q[i].astype(F32) * egcol[i]], axis=0).astype(BF16)
              for i in n]
        ak = [jnp.concatenate([attn[i], kdt[i]], axis=0).astype(BF16) for i in n]
        state = [s_ref[h] for h in range(HEADS)]
        for j in range(group):
            idx = [j * HEADS + h for h in range(HEADS)]
            ws = [jnp.dot(wq[i], state[h].astype(BF16), preferred_element_type=F32)
                  for h, i in enumerate(idx)]
            v_new = [sol[i][:, :HEAD_DIM] - ws[h][:c] for h, i in enumerate(idx)]
            r2 = [jnp.dot(ak[i], v_new[h].astype(BF16), preferred_element_type=F32)
                  for h, i in enumerate(idx)]
            state = [state[h] * jnp.exp(glast[i]) + r2[h][c:] for h, i in enumerate(idx)]
            for h in range(HEADS):
                z = z_ref[rows[j], hs[h]].astype(F32)
                o = _rms(ws[h][c:] + r2[h][:c], onorm) * _silu(z)
                o_ref[rows[j], hs[h]] = o.astype(o_ref.dtype)
        for h in range(HEADS):
            s_ref[h] = state[h]
        return 0

    lax.fori_loop(0, n_chunks // group, chunks, 0)


def _gdn_chunks(q, k, v, z, gc, bc, gt, o_norm, batch, seq, ts=1024, group=2):
    m = q.shape[0]
    nh = HEADS * HEAD_DIM
    tps = seq // ts
    assert (ts // GDN_CHUNK) % group == 0
    tok = lambda b, t: (b * tps + t, 0)
    return pl.pallas_call(
        functools.partial(_gdn_chunk_kernel, group=group),
        grid=(batch, tps),
        in_specs=[
            pl.BlockSpec((ts, nh), tok),
            pl.BlockSpec((ts, nh), tok),
            pl.BlockSpec((ts, nh), tok),
            pl.BlockSpec((ts, nh), tok),
            pl.BlockSpec((ts, 128), tok),
            pl.BlockSpec((ts, 128), tok),
            pl.BlockSpec((ts // GDN_CHUNK, HEADS, GDN_CHUNK), lambda b, t: (b * tps + t, 0, 0)),
            pl.BlockSpec((1, HEAD_DIM), lambda b, t: (0, 0)),
        ],
        out_specs=pl.BlockSpec((ts, nh), tok),
        out_shape=jax.ShapeDtypeStruct((m, nh), BF16),
        scratch_shapes=[pltpu.VMEM((HEADS, HEAD_DIM, HEAD_DIM), F32)],
        compiler_params=_cparams(2),
        name="gdn_chunks",
    )(q, k, v, z, gc, bc, gt, o_norm)


def _attention_mixer(h2d, rel_bias, norm_g, w_qkv, batch, seq):
    nh = HEADS * HEAD_DIM
    w_qk = w_qkv[:, :2 * nh].astype(BF16)
    w_vt = w_qkv[:, 2 * nh:].T.astype(BF16)
    qk, vt = _qkv_proj(h2d, norm_g.reshape(1, -1), w_qk, w_vt, batch, seq)
    o = _moba_attention(qk.reshape(batch, seq, 2 * nh), vt, _rel_bias_tiles(rel_bias), batch, seq)
    return o.reshape(batch * seq, nh)


def _gdn_mixer(h2d, norm_g, w_in, conv_w, a_log, dt_bias, o_norm, batch, seq):
    nh = HEADS * HEAD_DIM
    w_main = w_in[:, :4 * nh].astype(BF16)
    w_b = w_in[:, 4 * nh:4 * nh + HEADS]
    w_a = w_in[:, 4 * nh + HEADS:]
    pad = jnp.zeros((D_MODEL, 128 - HEADS), F32)
    w_ab = jnp.concatenate([w_b, pad, w_a, pad], axis=1)
    w_ab_hi = w_ab.astype(BF16)
    w_ab_lo = (w_ab - w_ab_hi.astype(F32)).astype(BF16)
    pad_r = jnp.zeros((128 - HEADS,), F32)
    alog_r = jnp.concatenate([a_log, pad_r]).reshape(1, 128)
    dt_r = jnp.concatenate([dt_bias, pad_r]).reshape(1, 128)
    q, k, v, z, gc, bc, gt = _gdn_front(h2d, norm_g.reshape(1, -1), w_main, w_ab_hi, w_ab_lo, conv_w,
                                        alog_r, dt_r, seq)
    return _gdn_chunks(q, k, v, z, gc, bc, gt, o_norm.reshape(1, -1), batch, seq)


def _out_proj_ffn(h2d, mixed, w_o, norm_g, w_up, conv_w, conv_b, w_down, final_g, seq, final_norm):
    return _mixer_out_ffn(h2d, mixed, w_o.astype(BF16), norm_g.reshape(1, -1), w_up[:, :D_FF].astype(BF16),
                          w_up[:, D_FF:].astype(BF16), conv_w, conv_b, w_down.astype(BF16),
                          final_g.reshape(1, -1), seq, final_norm)


def kernel(x, rel_bias, attn_norm, attn_w_qkv, attn_w_o, gdn_norm, gdn_w_in, gdn_conv_w, gdn_a_log,
           gdn_dt_bias, gdn_o_norm, gdn_w_o, ffn_norm, ffn_w_up, ffn_conv_w, ffn_conv_b, ffn_w_down,
           final_norm):
    batch, seq, d = x.shape
    h = x.reshape(batch * seq, d)
    mixed = _attention_mixer(h, rel_bias, attn_norm[0], attn_w_qkv[0], batch, seq)
    h = _out_proj_ffn(h, mixed, attn_w_o[0], ffn_norm[0], ffn_w_up[0], ffn_conv_w[0], ffn_conv_b[0],
                      ffn_w_down[0], final_norm, seq, False)
    mixed = _gdn_mixer(h, gdn_norm[0], gdn_w_in[0], gdn_conv_w[0], gdn_a_log[0], gdn_dt_bias[0],
                       gdn_o_norm[0], batch, seq)
    h = _out_proj_ffn(h, mixed, gdn_w_o[0], ffn_norm[1], ffn_w_up[1], ffn_conv_w[1], ffn_conv_b[1],
                      ffn_w_down[1], final_norm, seq, True)
    return h.reshape(batch, seq, d)
```
